```python
import math
import jax, jax.numpy as jnp
from jax import lax
import numpy as np

D_MODEL = 4096
BATCH = 2
SEQ = 8192
DEPTH = 1
DEC_BATCH = 8
DEC_SEQ = 2048
PAST_LEN = 128

D_FF = 11008
D_CONV = 2048
CONV_WIDTH = 31
HEAD_DIM = 128
HEADS_PER_GROUP = 8
DILATED_GROUPS = ((128, 1), (512, 4), (2048, 16))
N_GROUPS = len(DILATED_GROUPS)
N_ATTN_HEADS = HEADS_PER_GROUP * N_GROUPS
D_ATTN = N_ATTN_HEADS * HEAD_DIM
D_ATTN_OUT = HEADS_PER_GROUP * HEAD_DIM
N_BRANCHES = 2
C_Q = 2 * D_CONV
C_K = C_Q + D_ATTN
C_V = C_K + D_ATTN
C_GATE = C_V + D_ATTN
D_IN = C_GATE + N_BRANCHES * D_MODEL
D_PLE = 256
NUM_BUCKETS = 32
MAX_DISTANCE = 1024
N_POST_NORMS = 4
LN_EPS = 1e-5
NEG_INF = -1e30
DEEPNORM_ALPHA = (2.0 * DEPTH) ** 0.25
DEEPNORM_BETA = (8.0 * DEPTH) ** -0.25

kernel_name = 'hybrid_conv_dilated_attn_encoder'


def layer_norm(x, g, b):
    xf = x.astype(jnp.float32)
    mu = jnp.mean(xf, axis=-1, keepdims=True)
    var = jnp.mean(jnp.square(xf - mu), axis=-1, keepdims=True)
    return ((xf - mu) * lax.rsqrt(var + LN_EPS) * g + b).astype(x.dtype)


def swiglu(x, w_gate, w_up, w_down):
    return (jax.nn.silu(x @ w_gate) * (x @ w_up)) @ w_down


def rel_bucket(rel):
    half = NUM_BUCKETS // 2
    max_exact = half // 2
    ret = (rel > 0).astype(np.int32) * half
    n = np.abs(rel)
    large = max_exact + (np.log(np.maximum(n, 1) / max_exact) / np.log(MAX_DISTANCE / max_exact)
                         * (half - max_exact)).astype(np.int32)
    large = np.minimum(large, half - 1)
    return (ret + np.where(n < max_exact, n, large)).astype(np.int32)


def conv_module(u_glu, dw, dw_b, g, b, w_o):
    a, gt = jnp.split(u_glu, 2, axis=-1)
    h = a * jax.nn.sigmoid(gt)
    pad = CONV_WIDTH // 2
    h = lax.conv_general_dilated(h, dw[:, None, :], window_strides=(1,), padding=[(pad, pad)],
                                 dimension_numbers=('NWC', 'WIO', 'NWC'),
                                 feature_group_count=D_CONV) + dw_b
    h = jax.nn.silu(layer_norm(h, g, b))
    return h @ w_o


def dilated_window_attention(q, k, v, bias_table, window, dilation):
    B, S, H, E = q.shape
    R = window // 2 // dilation
    QB = R
    L = S // dilation
    nb = -(-L // QB)
    Lp = nb * QB

    def dec(t):
        return t.reshape(B, L, dilation, H, E)

    qd = jnp.pad(dec(q), ((0, 0), (0, Lp - L), (0, 0), (0, 0), (0, 0))).reshape(B, nb, QB, dilation, H, E)

    def kblocks(t):
        tp = jnp.pad(dec(t), ((0, 0), (QB, Lp - L + QB), (0, 0), (0, 0), (0, 0)))
        tp = tp.reshape(B, nb + 2, QB, dilation, H, E)
        return jnp.concatenate([tp[:, :-2], tp[:, 1:-1], tp[:, 2:]], axis=2)

    kb, vb = kblocks(k), kblocks(v)

    qi = np.arange(QB)[:, None]
    kj = np.arange(3 * QB)[None, :]
    rel = kj - QB - qi
    key_pos = np.arange(nb)[:, None, None] * QB + kj[None] - QB
    valid = (np.abs(rel)[None] <= R) & (key_pos >= 0) & (key_pos < L)
    bias = jnp.moveaxis(bias_table[rel_bucket(rel * dilation)], -1, 0).astype(jnp.float32)

    logits = jnp.einsum('bnqrhe,bnkrhe->bnrhqk', qd, kb,
                        preferred_element_type=jnp.float32) * (E ** -0.5)
    logits = jnp.where(valid[None, :, None, None], logits + bias, NEG_INF)
    lse = jax.nn.logsumexp(logits, axis=-1)
    probs = jnp.exp(logits - lse[..., None]).astype(v.dtype)
    o = jnp.einsum('bnrhqk,bnkrhe->bnqrhe', probs, vb)
    o = o.reshape(B, Lp, dilation, H, E)[:, :L].reshape(B, S, H, E)
    lse = jnp.transpose(lse, (0, 1, 4, 2, 3)).reshape(B, Lp, dilation, H)[:, :L].reshape(B, S, H)
    return o, lse


def encoder_layer(x, p_emb, rel_bias, ln_g, ln_b, w_ff1_gate, w_ff1_up, w_ff1_down, w_in, b_in,
                  conv_dw, conv_dw_b, conv_ln_g, conv_ln_b, w_conv_out, w_attn_out, w_out,
                  w_ff2_gate, w_ff2_up, w_ff2_down, w_ple, w_ple_gate, b_ple_gate):
    B, S, _ = x.shape
    x = layer_norm(DEEPNORM_ALPHA * x + 0.5 * swiglu(x, w_ff1_gate, w_ff1_up, w_ff1_down), ln_g[0], ln_b[0])

    u = x @ w_in + b_in
    u_glu, q, k, v, gates = jnp.split(u, [C_Q, C_K, C_V, C_GATE], axis=-1)

    conv_out = conv_module(u_glu, conv_dw, conv_dw_b, conv_ln_g, conv_ln_b, w_conv_out)

    q = q.reshape(B, S, N_ATTN_HEADS, HEAD_DIM)
    k = k.reshape(B, S, N_ATTN_HEADS, HEAD_DIM)
    v = v.reshape(B, S, N_ATTN_HEADS, HEAD_DIM)
    outs, lses = [], []
    for g, (win, dil) in enumerate(DILATED_GROUPS):
        hs = slice(g * HEADS_PER_GROUP, (g + 1) * HEADS_PER_GROUP)
        o, l = dilated_window_attention(q[:, :, hs], k[:, :, hs], v[:, :, hs], rel_bias[:, hs], win, dil)
        outs.append(o)
        lses.append(l)
    wts = jax.nn.softmax(jnp.stack(lses), axis=0).astype(x.dtype)
    attn = jnp.sum(wts[..., None] * jnp.stack(outs), axis=0).reshape(B, S, D_ATTN_OUT)
    attn_out = attn @ w_attn_out

    gate = jax.nn.sigmoid(gates.reshape(B, S, N_BRANCHES, D_MODEL))
    merged = gate[:, :, 0] * conv_out + gate[:, :, 1] * attn_out
    x = layer_norm(DEEPNORM_ALPHA * x + merged @ w_out, ln_g[1], ln_b[1])

    x = layer_norm(DEEPNORM_ALPHA * x + 0.5 * swiglu(x, w_ff2_gate, w_ff2_up, w_ff2_down), ln_g[2], ln_b[2])

    ple = (p_emb @ w_ple) * jax.nn.sigmoid(x @ w_ple_gate + b_ple_gate)
    return layer_norm(DEEPNORM_ALPHA * x + ple, ln_g[3], ln_b[3])


def setup_inputs(seed: int = 0) -> dict:
    key = jax.random.key(seed)
    ks = jax.random.split(key, 32)
    f32 = jnp.float32

    def nrm(k, shape, scale):
        return jax.random.normal(k, shape, f32) * scale

    return {
        'x_prompt': nrm(ks[0], (BATCH, SEQ, D_MODEL), 1.0),
        'x_sample': nrm(ks[1], (DEC_BATCH, DEC_SEQ, D_MODEL), 1.0),
        'p_prompt': nrm(ks[2], (DEPTH, BATCH, SEQ, D_PLE), 1.0),
        'p_sample': nrm(ks[3], (DEPTH, DEC_BATCH, DEC_SEQ, D_PLE), 1.0),
        'rel_bias': nrm(ks[4], (NUM_BUCKETS, N_ATTN_HEADS), 0.5),
        'ln_g': 1.0 + nrm(ks[5], (DEPTH, N_POST_NORMS, D_MODEL), 0.02),
        'ln_b': nrm(ks[6], (DEPTH, N_POST_NORMS, D_MODEL), 0.02),
        'w_ff1_gate': nrm(ks[7], (DEPTH, D_MODEL, D_FF), D_MODEL ** -0.5),
        'w_ff1_up': nrm(ks[8], (DEPTH, D_MODEL, D_FF), D_MODEL ** -0.5),
        'w_ff1_down': nrm(ks[9], (DEPTH, D_FF, D_MODEL), DEEPNORM_BETA * D_FF ** -0.5),
        'w_in': nrm(ks[10], (DEPTH, D_MODEL, D_IN), D_MODEL ** -0.5),
        'b_in': nrm(ks[11], (DEPTH, D_IN), 0.02),
        'conv_dw': nrm(ks[12], (DEPTH, CONV_WIDTH, D_CONV), CONV_WIDTH ** -0.5),
        'conv_dw_b': nrm(ks[13], (DEPTH, D_CONV), 0.02),
        'conv_ln_g': 1.0 + nrm(ks[14], (DEPTH, D_CONV), 0.02),
        'conv_ln_b': nrm(ks[15], (DEPTH, D_CONV), 0.02),
        'w_conv_out': nrm(ks[16], (DEPTH, D_CONV, D_MODEL), DEEPNORM_BETA * D_CONV ** -0.5),
        'w_attn_out': nrm(ks[17], (DEPTH, D_ATTN_OUT, D_MODEL), DEEPNORM_BETA * D_ATTN_OUT ** -0.5),
        'w_out': nrm(ks[18], (DEPTH, D_MODEL, D_MODEL), DEEPNORM_BETA * D_MODEL ** -0.5),
        'w_ff2_gate': nrm(ks[19], (DEPTH, D_MODEL, D_FF), D_MODEL ** -0.5),
        'w_ff2_up': nrm(ks[20], (DEPTH, D_MODEL, D_FF), D_MODEL ** -0.5),
        'w_ff2_down': nrm(ks[21], (DEPTH, D_FF, D_MODEL), DEEPNORM_BETA * D_FF ** -0.5),
        'w_ple': nrm(ks[22], (DEPTH, D_PLE, D_MODEL), DEEPNORM_BETA * D_PLE ** -0.5),
        'w_ple_gate': nrm(ks[23], (DEPTH, D_MODEL, D_MODEL), D_MODEL ** -0.5),
        'b_ple_gate': nrm(ks[24], (DEPTH, D_MODEL), 0.02),
    }


def reference(x_prompt, x_sample, p_prompt, p_sample, rel_bias, ln_g, ln_b, w_ff1_gate, w_ff1_up,
              w_ff1_down, w_in, b_in, conv_dw, conv_dw_b, conv_ln_g, conv_ln_b, w_conv_out,
              w_attn_out, w_out, w_ff2_gate, w_ff2_up, w_ff2_down, w_ple, w_ple_gate, b_ple_gate):
    def run(x, p):
        for i in range(DEPTH):
            x = encoder_layer(x, p[i], rel_bias, ln_g[i], ln_b[i], w_ff1_gate[i], w_ff1_up[i],
                              w_ff1_down[i], w_in[i], b_in[i], conv_dw[i], conv_dw_b[i],
                              conv_ln_g[i], conv_ln_b[i], w_conv_out[i], w_attn_out[i], w_out[i],
                              w_ff2_gate[i], w_ff2_up[i], w_ff2_down[i], w_ple[i], w_ple_gate[i],
                              b_ple_gate[i])
        return x

    y_prompt = run(x_prompt, p_prompt)
    y_sample = run(x_sample, p_sample)
    return (y_prompt, y_sample)
```

```python
import functools
import math

import jax
import jax.numpy as jnp
import numpy as np
from jax import lax
from jax.experimental import pallas as pl
from jax.experimental.pallas import tpu as pltpu

F32 = jnp.float32
BF16 = jnp.bfloat16

HEAD_DIM = 128
HEADS_PER_GROUP = 8
DILATED_GROUPS = ((128, 1), (512, 4), (2048, 16))
N_GROUPS = len(DILATED_GROUPS)
D_GROUP = HEADS_PER_GROUP * HEAD_DIM
D_ATTN = N_GROUPS * D_GROUP
CONV_WIDTH = 31
NUM_BUCKETS = 32
MAX_DISTANCE = 1024
LN_EPS = 1e-5
NEG_INF = -1e30

LANES = 128
SUBLANES_F32 = 8
ROWS_BF16_TILE = 16
VMEM_LIMIT_BYTES = 56 * 1024 * 1024

ATT_R = 64
ATT_TQ = 128
CONV_HALO = 16
CONV_ROWS = 16
ACC_CHUNK_COLS = 512
EPILOGUE_ROWS = 32


def _tile(n, target, align=LANES):
    if n <= target:
        return n
    t = (target // align) * align
    while t >= align:
        if n % t == 0:
            return t
        t -= align
    raise ValueError(f"no tile for {n} <= {target} aligned to {align}")


def _params(*sem):
    return pltpu.CompilerParams(dimension_semantics=sem, vmem_limit_bytes=VMEM_LIMIT_BYTES)


def _layer_norm(r, g, b):
    mu = jnp.mean(r, axis=-1, keepdims=True)
    c = r - mu
    var = jnp.mean(c * c, axis=-1, keepdims=True)
    return c * lax.rsqrt(var + LN_EPS) * g + b


def _sigmoid(x):
    return 1.0 / (1.0 + jnp.exp(-x))


def _for_row_chunks(n_rows, fn):
    def body(c, carry):
        fn(pl.ds(pl.multiple_of(c * EPILOGUE_ROWS, EPILOGUE_ROWS), EPILOGUE_ROWS))
        return carry
    lax.fori_loop(0, n_rows // EPILOGUE_ROWS, body, 0)


def _accumulate_dot(acc_ref, a, w_ref):
    n = acc_ref.shape[1]
    tn = _tile(n, ACC_CHUNK_COLS)
    for c in range(n // tn):
        cols = slice(c * tn, (c + 1) * tn)
        acc_ref[:, cols] += jnp.dot(a, w_ref[:, cols], preferred_element_type=F32)


def _ffn_kernel(x_ref, wg_ref, wu_ref, wd_ref, g_ref, b_ref, o32_ref, o16_ref, xb_ref, *, alpha, nf):
    f = pl.program_id(1)

    @pl.when(f == 0)
    def _():
        def init(rows):
            xb_ref[rows, :] = x_ref[rows, :].astype(BF16)
            o32_ref[rows, :] = jnp.zeros((EPILOGUE_ROWS, o32_ref.shape[1]), F32)
        _for_row_chunks(o32_ref.shape[0], init)

    xb = xb_ref[...]
    gt = jnp.dot(xb, wg_ref[...], preferred_element_type=F32)
    up = jnp.dot(xb, wu_ref[...], preferred_element_type=F32)
    h = (gt * _sigmoid(gt)) * up
    _accumulate_dot(o32_ref, h.astype(BF16), wd_ref)

    @pl.when(f == nf - 1)
    def _():
        def finish(rows):
            y = _layer_norm(alpha * x_ref[rows, :] + 0.5 * o32_ref[rows, :], g_ref[...], b_ref[...])
            o32_ref[rows, :] = y
            o16_ref[rows, :] = y.astype(BF16)
        _for_row_chunks(o32_ref.shape[0], finish)


def _ffn(x, wg, wu, wd, g, b, alpha):
    M, D = x.shape
    F = wg.shape[1]
    tm = _tile(M, 512, ROWS_BF16_TILE)
    tf = _tile(F, 256)
    nf = F // tf
    return pl.pallas_call(
        functools.partial(_ffn_kernel, alpha=alpha, nf=nf),
        grid=(M // tm, nf),
        in_specs=[
            pl.BlockSpec((tm, D), lambda i, f: (i, 0), pipeline_mode=pl.Buffered(1)),
            pl.BlockSpec((D, tf), lambda i, f: (0, f)),
            pl.BlockSpec((D, tf), lambda i, f: (0, f)),
            pl.BlockSpec((tf, D), lambda i, f: (f, 0)),
            pl.BlockSpec((1, D), lambda i, f: (0, 0)),
            pl.BlockSpec((1, D), lambda i, f: (0, 0)),
        ],
        out_specs=[
            pl.BlockSpec((tm, D), lambda i, f: (i, 0)),
            pl.BlockSpec((tm, D), lambda i, f: (i, 0)),
        ],
        out_shape=[jax.ShapeDtypeStruct((M, D), F32), jax.ShapeDtypeStruct((M, D), BF16)],
        scratch_shapes=[pltpu.VMEM((tm, D), BF16)],
        compiler_params=_params("parallel", "arbitrary"),
        name="ffn",
    )(x, wg, wu, wd, g, b)


def _glu_proj_kernel(x_ref, wa_ref, wg_ref, ba_ref, bg_ref, o_ref):
    x = x_ref[...]
    a = jnp.dot(x, wa_ref[...], preferred_element_type=F32) + ba_ref[...]
    gt = jnp.dot(x, wg_ref[...], preferred_element_type=F32) + bg_ref[...]
    o_ref[...] = a * _sigmoid(gt)


def _glu_proj(xb, w_in, b_in, d_conv):
    M, D = xb.shape
    tm = _tile(M, 1024, ROWS_BF16_TILE)
    tn = _tile(d_conv, 512)
    off = d_conv // tn
    return pl.pallas_call(
        _glu_proj_kernel,
        grid=(M // tm, d_conv // tn),
        in_specs=[
            pl.BlockSpec((tm, D), lambda i, j: (i, 0)),
            pl.BlockSpec((D, tn), lambda i, j: (0, j)),
            pl.BlockSpec((D, tn), lambda i, j: (0, j + off)),
            pl.BlockSpec((1, tn), lambda i, j: (0, j)),
            pl.BlockSpec((1, tn), lambda i, j: (0, j + off)),
        ],
        out_specs=pl.BlockSpec((tm, tn), lambda i, j: (i, j)),
        out_shape=jax.ShapeDtypeStruct((M, d_conv), F32),
        compiler_params=_params("parallel", "arbitrary"),
        name="glu_proj",
    )(xb, w_in, w_in, b_in, b_in)


def _qkv_proj_kernel(x_ref, w_ref, b_ref, o_ref):
    o_ref[...] = (jnp.dot(x_ref[...], w_ref[...], preferred_element_type=F32) + b_ref[...]).astype(o_ref.dtype)


def _qkv_proj(xb, w_in, b_in, col0):
    M, D = xb.shape
    N = 3 * D_ATTN
    tm = _tile(M, 1024, ROWS_BF16_TILE)
    tn = _tile(math.gcd(N, col0), 512)
    off = col0 // tn
    return pl.pallas_call(
        _qkv_proj_kernel,
        grid=(M // tm, N // tn),
        in_specs=[
            pl.BlockSpec((tm, D), lambda i, j: (i, 0)),
            pl.BlockSpec((D, tn), lambda i, j: (0, j + off)),
            pl.BlockSpec((1, tn), lambda i, j: (0, j + off)),
        ],
        out_specs=pl.BlockSpec((tm, tn), lambda i, j: (i, j)),
        out_shape=jax.ShapeDtypeStruct((M, N), BF16),
        compiler_params=_params("parallel", "arbitrary"),
        name="qkv_proj",
    )(xb, w_in, b_in)


def _conv_kernel(prev_ref, cur_ref, next_ref, dw_ref, dwb_ref, g_ref, b_ref, o_ref, buf_ref, *, ts, n_tiles):
    i = pl.program_id(1)
    buf_ref[0:CONV_HALO, :] = jnp.where(i > 0, prev_ref[0], 0.0)
    buf_ref[CONV_HALO:CONV_HALO + ts, :] = cur_ref[0]
    buf_ref[CONV_HALO + ts:, :] = jnp.where(i < n_tiles - 1, next_ref[0], 0.0)
    pad = CONV_WIDTH // 2
    for c in range(ts // CONV_ROWS):
        r0 = CONV_HALO + c * CONV_ROWS - pad
        acc = buf_ref[r0:r0 + CONV_ROWS, :] * dw_ref[0:1, :]
        for w in range(1, CONV_WIDTH):
            acc = acc + buf_ref[r0 + w:r0 + w + CONV_ROWS, :] * dw_ref[w:w + 1, :]
        y = _layer_norm(acc + dwb_ref[...], g_ref[...], b_ref[...])
        o_ref[0, c * CONV_ROWS:(c + 1) * CONV_ROWS, :] = (y * _sigmoid(y)).astype(o_ref.dtype)


def _conv_module(h, dw, dwb, g, b):
    B, S, C = h.shape
    ts = _tile(S, 128, CONV_HALO)
    n_tiles = S // ts
    hb = ts // CONV_HALO
    n_halo = S // CONV_HALO
    row = lambda bi, i: (0, 0)
    return pl.pallas_call(
        functools.partial(_conv_kernel, ts=ts, n_tiles=n_tiles),
        grid=(B, n_tiles),
        in_specs=[
            pl.BlockSpec((1, CONV_HALO, C), lambda bi, i: (bi, jnp.maximum(i * hb - 1, 0), 0)),
            pl.BlockSpec((1, ts, C), lambda bi, i: (bi, i, 0)),
            pl.BlockSpec((1, CONV_HALO, C), lambda bi, i: (bi, jnp.minimum((i + 1) * hb, n_halo - 1), 0)),
            pl.BlockSpec((CONV_WIDTH, C), row),
            pl.BlockSpec((1, C), row),
            pl.BlockSpec((1, C), row),
            pl.BlockSpec((1, C), row),
        ],
        out_specs=pl.BlockSpec((1, ts, C), lambda bi, i: (bi, i, 0)),
        out_shape=jax.ShapeDtypeStruct((B, S, C), BF16),
        scratch_shapes=[pltpu.VMEM((ts + 2 * CONV_HALO, C), F32)],
        compiler_params=_params("parallel", "arbitrary"),
        name="conv_module",
    )(h, h, h, dw, dwb, g, b)


def _t5_bucket(rel):
    half = NUM_BUCKETS // 2
    max_exact = half // 2
    ret = (rel > 0).astype(np.int32) * half
    n = np.abs(rel)
    large = max_exact + (np.log(np.maximum(n, 1) / max_exact) / np.log(MAX_DISTANCE / max_exact)
                         * (half - max_exact)).astype(np.int32)
    large = np.minimum(large, half - 1)
    return (ret + np.where(n < max_exact, n, large)).astype(np.int32)


def _band_bias(rel_bias, group, dilation):
    qi = np.arange(ATT_TQ)[:, None]
    kj = np.arange(ATT_TQ + 2 * ATT_R)[None, :]
    rel = kj - ATT_R - qi
    band = np.abs(rel) <= ATT_R
    idx = _t5_bucket(np.clip(rel, -ATT_R, ATT_R) * dilation)
    table = rel_bias[:, group * HEADS_PER_GROUP:(group + 1) * HEADS_PER_GROUP].astype(F32)
    bias = jnp.moveaxis(table[idx], -1, 0)
    return jnp.where(band[None], bias, NEG_INF)


def _attn_kernel(q_ref, kp_ref, kc_ref, kn_ref, vp_ref, vc_ref, vn_ref, bias_ref, o_ref, lse_ref,
                 kbuf_ref, vbuf_ref, *, sub_len, scale):
    i = pl.program_id(2)
    tq = ATT_TQ
    for buf, p, c, n in ((kbuf_ref, kp_ref, kc_ref, kn_ref), (vbuf_ref, vp_ref, vc_ref, vn_ref)):
        buf[0:ATT_R, :] = p[0]
        buf[ATT_R:ATT_R + tq, :] = c[0]
        buf[ATT_R + tq:, :] = n[0]
    key_pos = i * tq - ATT_R + lax.broadcasted_iota(jnp.int32, (1, tq + 2 * ATT_R), 1)
    valid = (key_pos >= 0) & (key_pos < sub_len)
    for h in range(HEADS_PER_GROUP):
        cols = slice(h * HEAD_DIM, (h + 1) * HEAD_DIM)
        s = lax.dot_general(q_ref[0, :, cols], kbuf_ref[:, cols], (((1,), (1,)), ((), ())),
                            preferred_element_type=F32) * scale
        s = jnp.where(valid, s + bias_ref[h], NEG_INF)
        m = jnp.max(s, axis=-1, keepdims=True)
        p = jnp.exp(s - m)
        l = jnp.sum(p, axis=-1, keepdims=True)
        p = p * (1.0 / l)
        o_ref[0, :, cols] = jnp.dot(p.astype(BF16), vbuf_ref[:, cols], preferred_element_type=F32)
        lse_ref[0, :, cols] = jnp.broadcast_to(m + jnp.log(l), (tq, HEAD_DIM))


def _attn_group(qkv, bias, B, S, group, dilation):
    L = S // dilation
    tq = ATT_TQ
    assert S % dilation == 0 and L % tq == 0, (S, dilation)
    n_col = 3 * D_ATTN // D_GROUP
    hb = tq // ATT_R
    n_halo = L // ATT_R
    view = qkv.reshape(B, L, dilation * 3 * D_ATTN)

    def cur(kind):
        return pl.BlockSpec((1, tq, D_GROUP), lambda b, r, i: (b, i, r * n_col + kind * N_GROUPS + group))

    def before(kind):
        return pl.BlockSpec((1, ATT_R, D_GROUP),
                            lambda b, r, i: (b, jnp.maximum(i * hb - 1, 0), r * n_col + kind * N_GROUPS + group))

    def after(kind):
        return pl.BlockSpec((1, ATT_R, D_GROUP),
                            lambda b, r, i: (b, jnp.minimum((i + 1) * hb, n_halo - 1),
                                             r * n_col + kind * N_GROUPS + group))

    out_spec = pl.BlockSpec((1, tq, D_GROUP), lambda b, r, i: (b, i, r))
    out_sds = jax.ShapeDtypeStruct((B, L, dilation * D_GROUP), F32)
    o, lse = pl.pallas_call(
        functools.partial(_attn_kernel, sub_len=L, scale=HEAD_DIM ** -0.5),
        grid=(B, dilation, L // tq),
        in_specs=[cur(0), before(1), cur(1), after(1), before(2), cur(2), after(2),
                  pl.BlockSpec((HEADS_PER_GROUP, tq, tq + 2 * ATT_R), lambda b, r, i: (0, 0, 0))],
        out_specs=[out_spec, out_spec],
        out_shape=[out_sds, out_sds],
        scratch_shapes=[pltpu.VMEM((tq + 2 * ATT_R, D_GROUP), BF16), pltpu.VMEM((tq + 2 * ATT_R, D_GROUP), BF16)],
        compiler_params=_params("parallel", "parallel", "arbitrary"),
        name=f"attn_d{dilation}",
    )(view, view, view, view, view, view, view, bias)
    return o.reshape(B * S, D_GROUP), lse.reshape(B * S, D_GROUP)


def _combine_kernel(o0_ref, o1_ref, o2_ref, l0_ref, l1_ref, l2_ref, out_ref):
    l0, l1, l2 = l0_ref[...], l1_ref[...], l2_ref[...]
    mx = jnp.maximum(jnp.maximum(l0, l1), l2)
    e0, e1, e2 = jnp.exp(l0 - mx), jnp.exp(l1 - mx), jnp.exp(l2 - mx)
    den = e0 + e1 + e2
    out = (e0 / den) * o0_ref[...] + (e1 / den) * o1_ref[...] + (e2 / den) * o2_ref[...]
    out_ref[...] = out.astype(out_ref.dtype)


def _combine(outs, lses):
    M, N = outs[0].shape
    tm = _tile(M, 256, ROWS_BF16_TILE)
    spec = pl.BlockSpec((tm, N), lambda i: (i, 0))
    return pl.pallas_call(
        _combine_kernel,
        grid=(M // tm,),
        in_specs=[spec] * 6,
        out_specs=spec,
        out_shape=jax.ShapeDtypeStruct((M, N), BF16),
        compiler_params=_params("parallel"),
        name="attn_combine",
    )(*outs, *lses)


def _merge_kernel(x_ref, hc_ref, at_ref, wg0_ref, wg1_ref, wc_ref, wa_ref, b0_ref, b1_ref, o_ref):
    x = x_ref[...]
    g0 = _sigmoid(jnp.dot(x, wg0_ref[...], preferred_element_type=F32) + b0_ref[...])
    g1 = _sigmoid(jnp.dot(x, wg1_ref[...], preferred_element_type=F32) + b1_ref[...])
    conv_out = jnp.dot(hc_ref[...], wc_ref[...], preferred_element_type=F32)
    attn_out = jnp.dot(at_ref[...], wa_ref[...], preferred_element_type=F32)
    o_ref[...] = (g0 * conv_out + g1 * attn_out).astype(o_ref.dtype)


def _merge(xb, hc, attn, w_in, b_in, w_conv_out, w_attn_out, col_gate):
    M, D = xb.shape
    tm = _tile(M, 1024, ROWS_BF16_TILE)
    tn = _tile(math.gcd(D, col_gate), 256)
    off0 = col_gate // tn
    off1 = (col_gate + D) // tn
    return pl.pallas_call(
        _merge_kernel,
        grid=(M // tm, D // tn),
        in_specs=[
            pl.BlockSpec((tm, D), lambda i, j: (i, 0)),
            pl.BlockSpec((tm, hc.shape[1]), lambda i, j: (i, 0)),
            pl.BlockSpec((tm, attn.shape[1]), lambda i, j: (i, 0)),
            pl.BlockSpec((D, tn), lambda i, j: (0, j + off0)),
            pl.BlockSpec((D, tn), lambda i, j: (0, j + off1)),
            pl.BlockSpec((hc.shape[1], tn), lambda i, j: (0, j)),
            pl.BlockSpec((attn.shape[1], tn), lambda i, j: (0, j)),
            pl.BlockSpec((1, tn), lambda i, j: (0, j + off0)),
            pl.BlockSpec((1, tn), lambda i, j: (0, j + off1)),
        ],
        out_specs=pl.BlockSpec((tm, tn), lambda i, j: (i, j)),
        out_shape=jax.ShapeDtypeStruct((M, D), BF16),
        compiler_params=_params("parallel", "arbitrary"),
        name="merge",
    )(xb, hc, attn, w_in, w_in, w_conv_out, w_attn_out, b_in, b_in)


def _out_proj_kernel(a_ref, w_ref, x_ref, g_ref, b_ref, o32_ref, o16_ref, *, alpha, nk):
    k = pl.program_id(1)

    @pl.when(k == 0)
    def _():
        o32_ref[...] = jnp.zeros_like(o32_ref)

    _accumulate_dot(o32_ref, a_ref[...], w_ref)

    @pl.when(k == nk - 1)
    def _():
        def finish(rows):
            y = _layer_norm(alpha * x_ref[rows, :] + o32_ref[rows, :], g_ref[...], b_ref[...])
            o32_ref[rows, :] = y
            o16_ref[rows, :] = y.astype(BF16)
        _for_row_chunks(o32_ref.shape[0], finish)


def _out_proj(a, w, x, g, b, alpha):
    M, K = a.shape
    D = w.shape[1]
    tm = _tile(M, 512, ROWS_BF16_TILE)
    tk = _tile(K, 1024)
    nk = K // tk
    return pl.pallas_call(
        functools.partial(_out_proj_kernel, alpha=alpha, nk=nk),
        grid=(M // tm, nk),
        in_specs=[
            pl.BlockSpec((tm, tk), lambda i, k: (i, k)),
            pl.BlockSpec((tk, D), lambda i, k: (k, 0)),
            pl.BlockSpec((tm, D), lambda i, k: (i, 0), pipeline_mode=pl.Buffered(1)),
            pl.BlockSpec((1, D), lambda i, k: (0, 0)),
            pl.BlockSpec((1, D), lambda i, k: (0, 0)),
        ],
        out_specs=[
            pl.BlockSpec((tm, D), lambda i, k: (i, 0)),
            pl.BlockSpec((tm, D), lambda i, k: (i, 0)),
        ],
        out_shape=[jax.ShapeDtypeStruct((M, D), F32), jax.ShapeDtypeStruct((M, D), BF16)],
        compiler_params=_params("parallel", "arbitrary"),
        name="out_proj",
    )(a, w, x, g, b)


def _ple_kernel(a_ref, w_ref, x_ref, p_ref, wp_ref, bg_ref, g_ref, b_ref, o_ref, *, alpha, nk):
    k = pl.program_id(1)

    @pl.when(k == 0)
    def _():
        o_ref[...] = jnp.zeros_like(o_ref)

    _accumulate_dot(o_ref, a_ref[...], w_ref)

    @pl.when(k == nk - 1)
    def _():
        def finish(rows):
            gate = _sigmoid(o_ref[rows, :] + bg_ref[...])
            ple = jnp.dot(p_ref[rows, :].astype(BF16), wp_ref[...], preferred_element_type=F32) * gate
            o_ref[rows, :] = _layer_norm(alpha * x_ref[rows, :] + ple, g_ref[...], b_ref[...])
        _for_row_chunks(o_ref.shape[0], finish)


def _ple(a, w, x, p, wp, bg, g, b, alpha):
    M, K = a.shape
    D = w.shape[1]
    P = p.shape[1]
    tm = _tile(M, 512, ROWS_BF16_TILE)
    tk = _tile(K, 1024)
    nk = K // tk
    vec = pl.BlockSpec((1, D), lambda i, k: (0, 0))
    return pl.pallas_call(
        functools.partial(_ple_kernel, alpha=alpha, nk=nk),
        grid=(M // tm, nk),
        in_specs=[
            pl.BlockSpec((tm, tk), lambda i, k: (i, k)),
            pl.BlockSpec((tk, D), lambda i, k: (k, 0)),
            pl.BlockSpec((tm, D), lambda i, k: (i, 0), pipeline_mode=pl.Buffered(1)),
            pl.BlockSpec((tm, P), lambda i, k: (i, 0)),
            pl.BlockSpec((P, D), lambda i, k: (0, 0)),
            vec, vec, vec,
        ],
        out_specs=pl.BlockSpec((tm, D), lambda i, k: (i, 0)),
        out_shape=jax.ShapeDtypeStruct((M, D), F32),
        compiler_params=_params("parallel", "arbitrary"),
        name="ple",
    )(a, w, x, p, wp, bg, g, b)


def _encoder_layer(x, p, B, S, w, biases, alpha):
    d_conv = w["conv_dw"].shape[1]
    col_q = 2 * d_conv
    col_gate = col_q + 3 * D_ATTN
    ln_g, ln_b = w["ln_g"], w["ln_b"]

    x1, x1b = _ffn(x, w["w_ff1_gate"], w["w_ff1_up"], w["w_ff1_down"], ln_g[0:1], ln_b[0:1], alpha)

    h = _glu_proj(x1b, w["w_in"], w["b_in"], d_conv)
    hc = _conv_module(h.reshape(B, S, d_conv), w["conv_dw"], w["conv_dw_b"], w["conv_ln_g"], w["conv_ln_b"])
    hc = hc.reshape(B * S, d_conv)

    qkv = _qkv_proj(x1b, w["w_in"], w["b_in"], col_q)
    outs, lses = [], []
    for gi, (_, dil) in enumerate(DILATED_GROUPS):
        o, lse = _attn_group(qkv, biases[gi], B, S, gi, dil)
        outs.append(o)
        lses.append(lse)
    attn = _combine(outs, lses)

    merged = _merge(x1b, hc, attn, w["w_in"], w["b_in"], w["w_conv_out"], w["w_attn_out"], col_gate)
    x2, x2b = _out_proj(merged, w["w_out"], x1, ln_g[1:2], ln_b[1:2], alpha)
    x3, x3b = _ffn(x2, w["w_ff2_gate"], w["w_ff2_up"], w["w_ff2_down"], ln_g[2:3], ln_b[2:3], alpha)
    del x2b
    return _ple(x3b, w["w_ple_gate"], x3, p, w["w_ple"], w["b_ple_gate"], ln_g[3:4], ln_b[3:4], alpha)


_MATMUL_WEIGHTS = ("w_ff1_gate", "w_ff1_up", "w_ff1_down", "w_in", "w_conv_out", "w_attn_out", "w_out",
                   "w_ff2_gate", "w_ff2_up", "w_ff2_down", "w_ple", "w_ple_gate")
_ROW_VECTORS = ("b_in", "conv_dw_b", "conv_ln_g", "conv_ln_b", "b_ple_gate")


def kernel(x_prompt, x_sample, p_prompt, p_sample, rel_bias, ln_g, ln_b, w_ff1_gate, w_ff1_up, w_ff1_down,
           w_in, b_in, conv_dw, conv_dw_b, conv_ln_g, conv_ln_b, w_conv_out, w_attn_out, w_out, w_ff2_gate,
           w_ff2_up, w_ff2_down, w_ple, w_ple_gate, b_ple_gate):
    stacked = dict(ln_g=ln_g, ln_b=ln_b, w_ff1_gate=w_ff1_gate, w_ff1_up=w_ff1_up, w_ff1_down=w_ff1_down,
                   w_in=w_in, b_in=b_in, conv_dw=conv_dw, conv_dw_b=conv_dw_b, conv_ln_g=conv_ln_g,
                   conv_ln_b=conv_ln_b, w_conv_out=w_conv_out, w_attn_out=w_attn_out, w_out=w_out,
                   w_ff2_gate=w_ff2_gate, w_ff2_up=w_ff2_up, w_ff2_down=w_ff2_down, w_ple=w_ple,
                   w_ple_gate=w_ple_gate, b_ple_gate=b_ple_gate)
    depth = ln_g.shape[0]
    alpha = (2.0 * depth) ** 0.25
    layers = []
    for i in range(depth):
        w = {k: v[i] for k, v in stacked.items()}
        for k in _MATMUL_WEIGHTS:
            w[k] = w[k].astype(BF16)
        for k in _ROW_VECTORS:
            w[k] = w[k].reshape(1, -1)
        layers.append(w)
    biases = [_band_bias(rel_bias, gi, dil) for gi, (_, dil) in enumerate(DILATED_GROUPS)]

    def run(x, p):
        B, S, D = x.shape
        y = x.reshape(B * S, D)
        for i, w in enumerate(layers):
            y = _encoder_layer(y, p[i].reshape(B * S, -1), B, S, w, biases, alpha)
        return y.reshape(B, S, D)

    return run(x_prompt, p_prompt), run(x_sample, p_sample)
```

```python
import functools
import math

import jax
import jax.numpy as jnp
import numpy as np
from jax import lax
from jax.experimental import pallas as pl
from jax.experimental.pallas import tpu as pltpu

F32 = jnp.float32
BF16 = jnp.bfloat16

HEAD_DIM = 128
HEADS_PER_GROUP = 8
DILATED_GROUPS = ((128, 1), (512, 4), (2048, 16))
N_GROUPS = len(DILATED_GROUPS)
D_GROUP = HEADS_PER_GROUP * HEAD_DIM
D_ATTN = N_GROUPS * D_GROUP
CONV_WIDTH = 31
NUM_BUCKETS = 32
MAX_DISTANCE = 1024
LN_EPS = 1e-5
NEG_INF = -1e30

LANES = 128
SUBLANES_F32 = 8
ROWS_BF16_TILE = 16
VMEM_LIMIT_BYTES = 58 * 1024 * 1024

ATT_R = 64
ATT_TQ = 128
CONV_HALO = 16
CONV_ROWS = 16
ACC_CHUNK_COLS = 512
EPILOGUE_ROWS = 32
NORM_SLABS = 8


def _tile(n, target, align=LANES):
    if n <= target:
        return n
    t = (target // align) * align
    while t >= align:
        if n % t == 0:
            return t
        t -= align
    raise ValueError(f"no tile for {n} <= {target} aligned to {align}")


def _params(*sem):
    return pltpu.CompilerParams(dimension_semantics=sem, vmem_limit_bytes=VMEM_LIMIT_BYTES)


def _layer_norm(r, g, b):
    mu = jnp.mean(r, axis=-1, keepdims=True)
    c = r - mu
    var = jnp.mean(c * c, axis=-1, keepdims=True)
    return c * lax.rsqrt(var + LN_EPS) * g + b


def _sigmoid(x):
    return 1.0 / (1.0 + jnp.exp(-x))


def _for_row_chunks(n_rows, fn):
    def body(c, carry):
        fn(pl.multiple_of(c * EPILOGUE_ROWS, EPILOGUE_ROWS))
        return carry
    lax.fori_loop(0, n_rows // EPILOGUE_ROWS, body, 0)


def _dot_into(acc_ref, a, w_ref, accumulate):
    n = acc_ref.shape[1]
    tn = _tile(n, ACC_CHUNK_COLS)
    for c in range(n // tn):
        cols = slice(c * tn, (c + 1) * tn)
        prod = jnp.dot(a, w_ref[:, cols], preferred_element_type=F32)
        acc_ref[:, cols] = acc_ref[:, cols] + prod if accumulate else prod


def _accumulate_steps(step, n_main, acc_ref, operand_fn, w_ref):
    @pl.when(step == 0)
    def _():
        _dot_into(acc_ref, operand_fn(), w_ref, accumulate=False)

    @pl.when((step > 0) & (step < n_main))
    def _():
        _dot_into(acc_ref, operand_fn(), w_ref, accumulate=True)


def _norm_steps(step, n_main, slab_rows, fn):
    @pl.when(step >= n_main)
    def _():
        slab0 = (step - n_main) * slab_rows
        _for_row_chunks(slab_rows, lambda r: fn(pl.ds(pl.multiple_of(slab0 + r, EPILOGUE_ROWS), EPILOGUE_ROWS),
                                                pl.ds(r, EPILOGUE_ROWS)))


def _slab_spec_factory(tm, width, n_main):
    rows = tm // NORM_SLABS

    def spec(cols=width):
        return pl.BlockSpec((rows, cols), lambda i, s: (i * NORM_SLABS + jnp.clip(s - n_main, 0, NORM_SLABS - 1), 0))
    return spec


def _ffn_kernel(xb_ref, wg_ref, wu_ref, wd_ref, x_ref, g_ref, b_ref, o32_ref, o16_ref, acc_ref, *, alpha, nf):
    f = pl.program_id(1)

    def hidden():
        xb = xb_ref[...]
        gt = jnp.dot(xb, wg_ref[...], preferred_element_type=F32)
        up = jnp.dot(xb, wu_ref[...], preferred_element_type=F32)
        return ((gt * _sigmoid(gt)) * up).astype(BF16)

    _accumulate_steps(f, nf, acc_ref, hidden, wd_ref)

    def finish(acc_rows, rows):
        y = _layer_norm(alpha * x_ref[rows, :] + 0.5 * acc_ref[acc_rows, :], g_ref[...], b_ref[...])
        o32_ref[rows, :] = y
        o16_ref[rows, :] = y.astype(BF16)

    _norm_steps(f, nf, o32_ref.shape[0], finish)


def _ffn(x, xb, wg, wu, wd, g, b, alpha):
    M, D = x.shape
    F = wg.shape[1]
    tm = _tile(M, 1024, ROWS_BF16_TILE * NORM_SLABS)
    tf = _tile(F, 256)
    nf = F // tf
    slab = _slab_spec_factory(tm, D, nf)
    last = nf - 1
    vec = pl.BlockSpec((1, D), lambda i, f: (0, 0))
    return pl.pallas_call(
        functools.partial(_ffn_kernel, alpha=alpha, nf=nf),
        grid=(M // tm, nf + NORM_SLABS),
        in_specs=[
            pl.BlockSpec((tm, D), lambda i, f: (i, 0), pipeline_mode=pl.Buffered(1)),
            pl.BlockSpec((D, tf), lambda i, f: (0, jnp.minimum(f, last))),
            pl.BlockSpec((D, tf), lambda i, f: (0, jnp.minimum(f, last))),
            pl.BlockSpec((tf, D), lambda i, f: (jnp.minimum(f, last), 0)),
            slab(), vec, vec,
        ],
        out_specs=[slab(), slab()],
        out_shape=[jax.ShapeDtypeStruct((M, D), F32), jax.ShapeDtypeStruct((M, D), BF16)],
        scratch_shapes=[pltpu.VMEM((tm, D), F32)],
        compiler_params=_params("parallel", "arbitrary"),
        name="ffn",
    )(xb, wg, wu, wd, x, g, b)


def _glu_proj_kernel(x_ref, wa_ref, wg_ref, ba_ref, bg_ref, o_ref):
    x = x_ref[...]
    a = jnp.dot(x, wa_ref[...], preferred_element_type=F32) + ba_ref[...]
    gt = jnp.dot(x, wg_ref[...], preferred_element_type=F32) + bg_ref[...]
    o_ref[...] = a * _sigmoid(gt)


def _glu_proj(xb, w_in, b_in, d_conv):
    M, D = xb.shape
    tm = _tile(M, 1024, ROWS_BF16_TILE)
    tn = _tile(d_conv, 512)
    off = d_conv // tn
    return pl.pallas_call(
        _glu_proj_kernel,
        grid=(M // tm, d_conv // tn),
        in_specs=[
            pl.BlockSpec((tm, D), lambda i, j: (i, 0)),
            pl.BlockSpec((D, tn), lambda i, j: (0, j)),
            pl.BlockSpec((D, tn), lambda i, j: (0, j + off)),
            pl.BlockSpec((1, tn), lambda i, j: (0, j)),
            pl.BlockSpec((1, tn), lambda i, j: (0, j + off)),
        ],
        out_specs=pl.BlockSpec((tm, tn), lambda i, j: (i, j)),
        out_shape=jax.ShapeDtypeStruct((M, d_conv), F32),
        compiler_params=_params("parallel", "arbitrary"),
        name="glu_proj",
    )(xb, w_in, w_in, b_in, b_in)


def _qkv_proj_kernel(x_ref, w_ref, b_ref, o_ref, *scratch, dilation):
    res = jnp.dot(x_ref[...], w_ref[...], preferred_element_type=F32) + b_ref[...]
    if dilation == 1:
        o_ref[0, 0] = res.astype(o_ref.dtype)
    else:
        (res_ref,) = scratch
        rows = res_ref.shape[1] // dilation
        for c in range(res_ref.shape[0]):
            lanes = slice(c * LANES, (c + 1) * LANES)
            res_ref[c] = res[:, lanes]
            for r in range(dilation):
                o_ref[0, r, :, lanes] = res_ref[c, pl.ds(r, rows, stride=dilation), :].astype(o_ref.dtype)


def _qkv_proj(xb, w_in, b_in, col_q, B, S, group, dilation):
    M, D = xb.shape
    L = S // dilation
    tm = _tile(S, 1024, ROWS_BF16_TILE * dilation)
    tn = _tile(math.gcd(D_GROUP, col_q), 512)
    per_kind = D_GROUP // tn
    tiles_per_seq = S // tm

    def w_col(i, j):
        kind = j // per_kind
        return (0, (col_q + kind * D_ATTN + group * D_GROUP) // tn + j % per_kind)

    return pl.pallas_call(
        functools.partial(_qkv_proj_kernel, dilation=dilation),
        grid=(M // tm, 3 * per_kind),
        in_specs=[
            pl.BlockSpec((tm, D), lambda i, j: (i, 0)),
            pl.BlockSpec((D, tn), w_col),
            pl.BlockSpec((1, tn), w_col),
        ],
        out_specs=pl.BlockSpec((1, dilation, tm // dilation, tn),
                               lambda i, j: (i // tiles_per_seq, 0, i % tiles_per_seq, j)),
        out_shape=jax.ShapeDtypeStruct((B, dilation, L, 3 * D_GROUP), BF16),
        scratch_shapes=[] if dilation == 1 else [pltpu.VMEM((tn // LANES, tm, LANES), F32)],
        compiler_params=_params("parallel", "arbitrary"),
        name=f"qkv_proj_d{dilation}",
    )(xb, w_in, b_in)


def _conv_kernel(prev_ref, cur_ref, next_ref, dw_ref, dwb_ref, g_ref, b_ref, o_ref, buf_ref, shift_ref,
                 *, ts, n_tiles):
    i = pl.program_id(1)
    buf_ref[0:CONV_HALO, :] = jnp.where(i > 0, prev_ref[0], 0.0)
    buf_ref[CONV_HALO:CONV_HALO + ts, :] = cur_ref[0]
    buf_ref[CONV_HALO + ts:, :] = jnp.where(i < n_tiles - 1, next_ref[0], 0.0)
    n_shift = shift_ref.shape[1]
    for s in range(1, SUBLANES_F32):
        shift_ref[s - 1] = buf_ref[s:s + n_shift, :]
    first = CONV_HALO - CONV_WIDTH // 2
    for c in range(ts // CONV_ROWS):
        acc = None
        for w in range(CONV_WIDTH):
            row = first + c * CONV_ROWS + w
            s, base = row % SUBLANES_F32, row - row % SUBLANES_F32
            src = buf_ref if s == 0 else shift_ref.at[s - 1]
            tap = jnp.concatenate([dw_ref[w]] * (CONV_ROWS // SUBLANES_F32), axis=0)
            term = src[base:base + CONV_ROWS, :] * tap
            acc = term if acc is None else acc + term
        y = _layer_norm(acc + dwb_ref[...], g_ref[...], b_ref[...])
        o_ref[0, c * CONV_ROWS:(c + 1) * CONV_ROWS, :] = (y * _sigmoid(y)).astype(o_ref.dtype)


def _conv_module(h, dw, dwb, g, b):
    B, S, C = h.shape
    ts = _tile(S, 128, CONV_HALO)
    n_tiles = S // ts
    hb = ts // CONV_HALO
    n_halo = S // CONV_HALO
    row = lambda bi, i: (0, 0)
    n_buf = ts + 2 * CONV_HALO
    taps = jnp.broadcast_to(dw[:, None, :], (CONV_WIDTH, SUBLANES_F32, C))
    return pl.pallas_call(
        functools.partial(_conv_kernel, ts=ts, n_tiles=n_tiles),
        grid=(B, n_tiles),
        in_specs=[
            pl.BlockSpec((1, CONV_HALO, C), lambda bi, i: (bi, jnp.maximum(i * hb - 1, 0), 0)),
            pl.BlockSpec((1, ts, C), lambda bi, i: (bi, i, 0)),
            pl.BlockSpec((1, CONV_HALO, C), lambda bi, i: (bi, jnp.minimum((i + 1) * hb, n_halo - 1), 0)),
            pl.BlockSpec((CONV_WIDTH, SUBLANES_F32, C), lambda bi, i: (0, 0, 0)),
            pl.BlockSpec((1, C), row),
            pl.BlockSpec((1, C), row),
            pl.BlockSpec((1, C), row),
        ],
        out_specs=pl.BlockSpec((1, ts, C), lambda bi, i: (bi, i, 0)),
        out_shape=jax.ShapeDtypeStruct((B, S, C), BF16),
        scratch_shapes=[pltpu.VMEM((n_buf, C), F32),
                        pltpu.VMEM((SUBLANES_F32 - 1, n_buf - SUBLANES_F32, C), F32)],
        compiler_params=_params("parallel", "arbitrary"),
        name="conv_module",
    )(h, h, h, taps, dwb, g, b)


def _t5_bucket(rel):
    half = NUM_BUCKETS // 2
    max_exact = half // 2
    ret = (rel > 0).astype(np.int32) * half
    n = np.abs(rel)
    large = max_exact + (np.log(np.maximum(n, 1) / max_exact) / np.log(MAX_DISTANCE / max_exact)
                         * (half - max_exact)).astype(np.int32)
    large = np.minimum(large, half - 1)
    return (ret + np.where(n < max_exact, n, large)).astype(np.int32)


def _band_bias(rel_bias, group, dilation):
    qi = np.arange(ATT_TQ)[:, None]
    kj = np.arange(ATT_TQ + 2 * ATT_R)[None, :]
    rel = kj - ATT_R - qi
    band = np.abs(rel) <= ATT_R
    bucket = _t5_bucket(np.clip(rel, -ATT_R, ATT_R) * dilation)
    table = rel_bias[:, group * HEADS_PER_GROUP:(group + 1) * HEADS_PER_GROUP].astype(F32)
    onehot = bucket[None, :, :, None] == np.arange(NUM_BUCKETS)
    bias = jnp.sum(jnp.where(onehot, table.T[:, None, None, :], 0.0), axis=-1)
    return jnp.where(band[None], bias, NEG_INF)


def _attn_kernel(q_ref, kp_ref, kc_ref, kn_ref, vp_ref, vc_ref, vn_ref, bias_ref, o_ref, lse_ref,
                 kbuf_ref, vbuf_ref, *, sub_len, scale):
    i = pl.program_id(2)
    tq = ATT_TQ
    for buf, p, c, n in ((kbuf_ref, kp_ref, kc_ref, kn_ref), (vbuf_ref, vp_ref, vc_ref, vn_ref)):
        buf[0:ATT_R, :] = p[0, 0]
        buf[ATT_R:ATT_R + tq, :] = c[0, 0]
        buf[ATT_R + tq:, :] = n[0, 0]
    key_pos = i * tq - ATT_R + lax.broadcasted_iota(jnp.int32, (1, tq + 2 * ATT_R), 1)
    valid = (key_pos >= 0) & (key_pos < sub_len)
    for h in range(HEADS_PER_GROUP):
        cols = slice(h * HEAD_DIM, (h + 1) * HEAD_DIM)
        s = lax.dot_general(q_ref[0, 0, :, cols], kbuf_ref[:, cols], (((1,), (1,)), ((), ())),
                            preferred_element_type=F32) * scale
        s = jnp.where(valid, s + bias_ref[h], NEG_INF)
        m = jnp.max(s, axis=-1, keepdims=True)
        p = jnp.exp(s - m)
        l = jnp.sum(p, axis=-1, keepdims=True)
        p = p * (1.0 / l)
        o_ref[0, 0, :, cols] = jnp.dot(p.astype(BF16), vbuf_ref[:, cols], preferred_element_type=F32)
        lse_ref[0, 0, :, cols] = jnp.broadcast_to(m + jnp.log(l), (tq, HEAD_DIM))


def _attn_group(qkv, bias):
    B, dilation, L, _ = qkv.shape
    tq = ATT_TQ
    assert L % tq == 0, (L, tq)
    hb = tq // ATT_R
    n_halo = L // ATT_R

    def cur(kind):
        return pl.BlockSpec((1, 1, tq, D_GROUP), lambda b, r, i: (b, r, i, kind))

    def before(kind):
        return pl.BlockSpec((1, 1, ATT_R, D_GROUP), lambda b, r, i: (b, r, jnp.maximum(i * hb - 1, 0), kind))

    def after(kind):
        return pl.BlockSpec((1, 1, ATT_R, D_GROUP),
                            lambda b, r, i: (b, r, jnp.minimum((i + 1) * hb, n_halo - 1), kind))

    out_spec = pl.BlockSpec((1, 1, tq, D_GROUP), lambda b, r, i: (b, r, i, 0))
    out_sds = jax.ShapeDtypeStruct((B, dilation, L, D_GROUP), F32)
    return pl.pallas_call(
        functools.partial(_attn_kernel, sub_len=L, scale=HEAD_DIM ** -0.5),
        grid=(B, dilation, L // tq),
        in_specs=[cur(0), before(1), cur(1), after(1), before(2), cur(2), after(2),
                  pl.BlockSpec((HEADS_PER_GROUP, tq, tq + 2 * ATT_R), lambda b, r, i: (0, 0, 0))],
        out_specs=[out_spec, out_spec],
        out_shape=[out_sds, out_sds],
        scratch_shapes=[pltpu.VMEM((tq + 2 * ATT_R, D_GROUP), BF16), pltpu.VMEM((tq + 2 * ATT_R, D_GROUP), BF16)],
        compiler_params=_params("parallel", "parallel", "arbitrary"),
        name=f"attn_d{dilation}",
    )(qkv, qkv, qkv, qkv, qkv, qkv, qkv, bias)


def _combine_kernel(*refs):
    n = N_GROUPS
    o_refs, l_refs, out_ref, buf_ref = refs[:n], refs[n:2 * n], refs[2 * n], refs[2 * n + 1]

    def token_order(ref, slot, lanes):
        dilation, rows = ref.shape[1], ref.shape[2]
        if dilation == 1:
            return ref[0, 0, :, lanes]
        for r in range(dilation):
            buf_ref[slot, pl.ds(r, rows, stride=dilation), :] = ref[0, r, :, lanes]
        return buf_ref[slot]

    for c in range(out_ref.shape[2] // LANES):
        lanes = slice(c * LANES, (c + 1) * LANES)
        outs = [token_order(ref, g, lanes) for g, ref in enumerate(o_refs)]
        lses = [token_order(ref, n + g, lanes) for g, ref in enumerate(l_refs)]
        mx = functools.reduce(jnp.maximum, lses)
        es = [jnp.exp(l - mx) for l in lses]
        den = functools.reduce(lambda a, b: a + b, es)
        acc = None
        for e, o in zip(es, outs):
            term = (e / den) * o
            acc = term if acc is None else acc + term
        out_ref[0, :, lanes] = acc.astype(out_ref.dtype)


def _combine(outs, lses, S):
    B = outs[0].shape[0]
    N = outs[0].shape[-1]
    max_dil = max(o.shape[1] for o in outs)
    tm = _tile(S, 256, SUBLANES_F32 * max_dil)

    def spec(a):
        d = a.shape[1]
        return pl.BlockSpec((1, d, tm // d, N), lambda b, i: (b, 0, i, 0))

    return pl.pallas_call(
        _combine_kernel,
        grid=(B, S // tm),
        in_specs=[spec(a) for a in (*outs, *lses)],
        out_specs=pl.BlockSpec((1, tm, N), lambda b, i: (b, i, 0)),
        out_shape=jax.ShapeDtypeStruct((B, S, N), BF16),
        scratch_shapes=[pltpu.VMEM((2 * N_GROUPS, tm, LANES), F32)],
        compiler_params=_params("parallel", "arbitrary"),
        name="attn_combine",
    )(*outs, *lses)


def _merge_kernel(x_ref, hc_ref, at_ref, wg0_ref, wg1_ref, wc_ref, wa_ref, b0_ref, b1_ref, o_ref):
    x = x_ref[...]
    g0 = _sigmoid(jnp.dot(x, wg0_ref[...], preferred_element_type=F32) + b0_ref[...])
    g1 = _sigmoid(jnp.dot(x, wg1_ref[...], preferred_element_type=F32) + b1_ref[...])
    conv_out = jnp.dot(hc_ref[...], wc_ref[...], preferred_element_type=F32)
    attn_out = jnp.dot(at_ref[...], wa_ref[...], preferred_element_type=F32)
    o_ref[...] = (g0 * conv_out + g1 * attn_out).astype(o_ref.dtype)


def _merge(xb, hc, attn, w_in, b_in, w_conv_out, w_attn_out, col_gate):
    M, D = xb.shape
    tm = _tile(M, 1024, ROWS_BF16_TILE)
    tn = _tile(math.gcd(D, col_gate), 256)
    off0 = col_gate // tn
    off1 = (col_gate + D) // tn
    return pl.pallas_call(
        _merge_kernel,
        grid=(M // tm, D // tn),
        in_specs=[
            pl.BlockSpec((tm, D), lambda i, j: (i, 0)),
            pl.BlockSpec((tm, hc.shape[1]), lambda i, j: (i, 0)),
            pl.BlockSpec((tm, attn.shape[1]), lambda i, j: (i, 0)),
            pl.BlockSpec((D, tn), lambda i, j: (0, j + off0)),
            pl.BlockSpec((D, tn), lambda i, j: (0, j + off1)),
            pl.BlockSpec((hc.shape[1], tn), lambda i, j: (0, j)),
            pl.BlockSpec((attn.shape[1], tn), lambda i, j: (0, j)),
            pl.BlockSpec((1, tn), lambda i, j: (0, j + off0)),
            pl.BlockSpec((1, tn), lambda i, j: (0, j + off1)),
        ],
        out_specs=pl.BlockSpec((tm, tn), lambda i, j: (i, j)),
        out_shape=jax.ShapeDtypeStruct((M, D), BF16),
        compiler_params=_params("parallel", "arbitrary"),
        name="merge",
    )(xb, hc, attn, w_in, w_in, w_conv_out, w_attn_out, b_in, b_in)


def _out_proj_kernel(a_ref, w_ref, x_ref, g_ref, b_ref, o32_ref, o16_ref, acc_ref, *, alpha, nk):
    k = pl.program_id(1)
    _accumulate_steps(k, nk, acc_ref, lambda: a_ref[...], w_ref)

    def finish(acc_rows, rows):
        y = _layer_norm(alpha * x_ref[rows, :] + acc_ref[acc_rows, :], g_ref[...], b_ref[...])
        o32_ref[rows, :] = y
        o16_ref[rows, :] = y.astype(BF16)

    _norm_steps(k, nk, o32_ref.shape[0], finish)


def _out_proj(a, w, x, g, b, alpha):
    M, K = a.shape
    D = w.shape[1]
    tm = _tile(M, 1024, ROWS_BF16_TILE * NORM_SLABS)
    tk = _tile(K, 1024)
    nk = K // tk
    last = nk - 1
    slab = _slab_spec_factory(tm, D, nk)
    vec = pl.BlockSpec((1, D), lambda i, k: (0, 0))
    return pl.pallas_call(
        functools.partial(_out_proj_kernel, alpha=alpha, nk=nk),
        grid=(M // tm, nk + NORM_SLABS),
        in_specs=[
            pl.BlockSpec((tm, tk), lambda i, k: (i, jnp.minimum(k, last))),
            pl.BlockSpec((tk, D), lambda i, k: (jnp.minimum(k, last), 0)),
            slab(), vec, vec,
        ],
        out_specs=[slab(), slab()],
        out_shape=[jax.ShapeDtypeStruct((M, D), F32), jax.ShapeDtypeStruct((M, D), BF16)],
        scratch_shapes=[pltpu.VMEM((tm, D), F32)],
        compiler_params=_params("parallel", "arbitrary"),
        name="out_proj",
    )(a, w, x, g, b)


def _ple_kernel(a_ref, w_ref, x_ref, p_ref, wp_ref, bg_ref, g_ref, b_ref, o_ref, acc_ref, *, alpha, nk):
    k = pl.program_id(1)
    _accumulate_steps(k, nk, acc_ref, lambda: a_ref[...], w_ref)

    def finish(acc_rows, rows):
        gate = _sigmoid(acc_ref[acc_rows, :] + bg_ref[...])
        ple = jnp.dot(p_ref[rows, :].astype(BF16), wp_ref[...], preferred_element_type=F32) * gate
        o_ref[rows, :] = _layer_norm(alpha * x_ref[rows, :] + ple, g_ref[...], b_ref[...])

    _norm_steps(k, nk, o_ref.shape[0], finish)


def _ple(a, w, x, p, wp, bg, g, b, alpha):
    M, K = a.shape
    D = w.shape[1]
    P = p.shape[1]
    tm = _tile(M, 1024, ROWS_BF16_TILE * NORM_SLABS)
    tk = _tile(K, 1024)
    nk = K // tk
    last = nk - 1
    slab = _slab_spec_factory(tm, D, nk)
    vec = pl.BlockSpec((1, D), lambda i, k: (0, 0))
    return pl.pallas_call(
        functools.partial(_ple_kernel, alpha=alpha, nk=nk),
        grid=(M // tm, nk + NORM_SLABS),
        in_specs=[
            pl.BlockSpec((tm, tk), lambda i, k: (i, jnp.minimum(k, last))),
            pl.BlockSpec((tk, D), lambda i, k: (jnp.minimum(k, last), 0)),
            slab(), slab(P),
            pl.BlockSpec((P, D), lambda i, k: (0, 0)),
            vec, vec, vec,
        ],
        out_specs=slab(),
        out_shape=jax.ShapeDtypeStruct((M, D), F32),
        scratch_shapes=[pltpu.VMEM((tm, D), F32)],
        compiler_params=_params("parallel", "arbitrary"),
        name="ple",
    )(a, w, x, p, wp, bg, g, b)


def _encoder_layer(x, p, B, S, w, biases, alpha):
    d_conv = w["conv_dw"].shape[1]
    col_q = 2 * d_conv
    col_gate = col_q + 3 * D_ATTN
    ln_g, ln_b = w["ln_g"], w["ln_b"]

    x1, x1b = _ffn(x, x.astype(BF16), w["w_ff1_gate"], w["w_ff1_up"], w["w_ff1_down"], ln_g[0:1], ln_b[0:1], alpha)

    h = _glu_proj(x1b, w["w_in"], w["b_in"], d_conv)
    hc = _conv_module(h.reshape(B, S, d_conv), w["conv_dw"], w["conv_dw_b"], w["conv_ln_g"], w["conv_ln_b"])
    hc = hc.reshape(B * S, d_conv)

    outs, lses = [], []
    for gi, (_, dil) in enumerate(DILATED_GROUPS):
        qkv = _qkv_proj(x1b, w["w_in"], w["b_in"], col_q, B, S, gi, dil)
        o, lse = _attn_group(qkv, biases[gi])
        outs.append(o)
        lses.append(lse)
    attn = _combine(outs, lses, S).reshape(B * S, D_GROUP)

    merged = _merge(x1b, hc, attn, w["w_in"], w["b_in"], w["w_conv_out"], w["w_attn_out"], col_gate)
    x2, x2b = _out_proj(merged, w["w_out"], x1, ln_g[1:2], ln_b[1:2], alpha)
    x3, x3b = _ffn(x2, x2b, w["w_ff2_gate"], w["w_ff2_up"], w["w_ff2_down"], ln_g[2:3], ln_b[2:3], alpha)
    return _ple(x3b, w["w_ple_gate"], x3, p, w["w_ple"], w["b_ple_gate"], ln_g[3:4], ln_b[3:4], alpha)


_MATMUL_WEIGHTS = ("w_ff1_gate", "w_ff1_up", "w_ff1_down", "w_in", "w_conv_out", "w_attn_out", "w_out",
                   "w_ff2_gate", "w_ff2_up", "w_ff2_down", "w_ple", "w_ple_gate")
_ROW_VECTORS = ("b_in", "conv_dw_b", "conv_ln_g", "conv_ln_b", "b_ple_gate")


def kernel(x_prompt, x_sample, p_prompt, p_sample, rel_bias, ln_g, ln_b, w_ff1_gate, w_ff1_up, w_ff1_down,
           w_in, b_in, conv_dw, conv_dw_b, conv_ln_g, conv_ln_b, w_conv_out, w_attn_out, w_out, w_ff2_gate,
           w_ff2_up, w_ff2_down, w_ple, w_ple_gate, b_ple_gate):
    stacked = dict(ln_g=ln_g, ln_b=ln_b, w_ff1_gate=w_ff1_gate, w_ff1_up=w_ff1_up, w_ff1_down=w_ff1_down,
                   w_in=w_in, b_in=b_in, conv_dw=conv_dw, conv_dw_b=conv_dw_b, conv_ln_g=conv_ln_g,
                   conv_ln_b=conv_ln_b, w_conv_out=w_conv_out, w_attn_out=w_attn_out, w_out=w_out,
                   w_ff2_gate=w_ff2_gate, w_ff2_up=w_ff2_up, w_ff2_down=w_ff2_down, w_ple=w_ple,
                   w_ple_gate=w_ple_gate, b_ple_gate=b_ple_gate)
    depth = ln_g.shape[0]
    alpha = (2.0 * depth) ** 0.25
    layers = []
    for i in range(depth):
        w = {k: v[i] for k, v in stacked.items()}
        for k in _MATMUL_WEIGHTS:
            w[k] = w[k].astype(BF16)
        for k in _ROW_VECTORS:
            w[k] = w[k].reshape(1, -1)
        layers.append(w)
    biases = [_band_bias(rel_bias, gi, dil) for gi, (_, dil) in enumerate(DILATED_GROUPS)]

    def run(x, p):
        B, S, D = x.shape
        y = x.reshape(B * S, D)
        for i, w in enumerate(layers):
            y = _encoder_layer(y, p[i].reshape(B * S, -1), B, S, w, biases, alpha)
        return y.reshape(B, S, D)

    return run(x_prompt, p_prompt), run(x_sample, p_sample)
```

```python
import functools
import math

import jax
import jax.numpy as jnp
import numpy as np
from jax import lax
from jax.experimental import pallas as pl
from jax.experimental.pallas import tpu as pltpu

F32 = jnp.float32
BF16 = jnp.bfloat16

HEAD_DIM = 128
HEADS_PER_GROUP = 8
DILATED_GROUPS = ((128, 1), (512, 4), (2048, 16))
N_GROUPS = len(DILATED_GROUPS)
D_GROUP = HEADS_PER_GROUP * HEAD_DIM
D_ATTN = N_GROUPS * D_GROUP
CONV_WIDTH = 31
NUM_BUCKETS = 32
MAX_DISTANCE = 1024
LN_EPS = 1e-5
NEG_INF = -1e30

LANES = 128
SUBLANES_F32 = 8
ROWS_BF16_TILE = 16
VMEM_LIMIT_BYTES = 58 * 1024 * 1024

ATT_R = 64
ATT_TQ = 128
CONV_HALO = 16
CONV_ROWS = 16
ACC_CHUNK_COLS = 512
EPILOGUE_ROWS = 64
NORM_SLABS = 8


def _tile(n, target, align=LANES):
    if n <= target:
        return n
    t = (target // align) * align
    while t >= align:
        if n % t == 0:
            return t
        t -= align
    raise ValueError(f"no tile for {n} <= {target} aligned to {align}")


def _params(*sem):
    return pltpu.CompilerParams(dimension_semantics=sem, vmem_limit_bytes=VMEM_LIMIT_BYTES)


def _layer_norm(r, g, b):
    mu = jnp.mean(r, axis=-1, keepdims=True)
    c = r - mu
    var = jnp.mean(c * c, axis=-1, keepdims=True)
    return c * lax.rsqrt(var + LN_EPS) * g + b


def _sigmoid(x):
    return 1.0 / (1.0 + jnp.exp(-x))


def _for_row_chunks(n_rows, fn):
    def body(c, carry):
        fn(pl.multiple_of(c * EPILOGUE_ROWS, EPILOGUE_ROWS))
        return carry
    lax.fori_loop(0, n_rows // EPILOGUE_ROWS, body, 0)


def _dot_into(acc_ref, a, w_ref, accumulate):
    n = acc_ref.shape[1]
    tn = _tile(n, ACC_CHUNK_COLS)
    for c in range(n // tn):
        cols = slice(c * tn, (c + 1) * tn)
        prod = jnp.dot(a, w_ref[:, cols], preferred_element_type=F32)
        acc_ref[:, cols] = acc_ref[:, cols] + prod if accumulate else prod


def _accumulate_steps(step, n_main, acc_ref, operand_fn, w_ref):
    @pl.when(step == 0)
    def _():
        _dot_into(acc_ref, operand_fn(), w_ref, accumulate=False)

    @pl.when((step > 0) & (step < n_main))
    def _():
        _dot_into(acc_ref, operand_fn(), w_ref, accumulate=True)


def _residual_steps(step, n_slabs, acc_ref, x_ref, scale):
    slab_rows = x_ref.shape[0]

    @pl.when(step < n_slabs)
    def _():
        slab0 = step * slab_rows

        def add(r):
            rows = pl.ds(pl.multiple_of(slab0 + r, EPILOGUE_ROWS), EPILOGUE_ROWS)
            acc_ref[rows, :] = acc_ref[rows, :] + scale * x_ref[pl.ds(r, EPILOGUE_ROWS), :]
        _for_row_chunks(slab_rows, add)


def _norm_steps(step, n_main, slab_rows, fn):
    @pl.when(step >= n_main)
    def _():
        slab0 = (step - n_main) * slab_rows
        _for_row_chunks(slab_rows, lambda r: fn(pl.ds(pl.multiple_of(slab0 + r, EPILOGUE_ROWS), EPILOGUE_ROWS),
                                                pl.ds(r, EPILOGUE_ROWS)))


def _slab_spec_factory(tm, width, n_main, n_slabs):
    rows = tm // n_slabs

    def late(cols=width):
        return pl.BlockSpec((rows, cols), lambda i, s: (i * n_slabs + jnp.clip(s - n_main, 0, n_slabs - 1), 0))

    def early(cols=width):
        return pl.BlockSpec((rows, cols), lambda i, s: (i * n_slabs + jnp.minimum(s, n_slabs - 1), 0))
    return late, early


def _ffn_kernel(xb_ref, wg_ref, wu_ref, wd_ref, x_ref, g_ref, b_ref, o32_ref, o16_ref, acc_ref,
                *, alpha, nf, n_slabs):
    f = pl.program_id(1)

    def hidden():
        xb = xb_ref[...]
        gt = jnp.dot(xb, wg_ref[...], preferred_element_type=F32)
        up = jnp.dot(xb, wu_ref[...], preferred_element_type=F32)
        return ((gt * _sigmoid(gt)) * up).astype(BF16)

    _accumulate_steps(f, nf, acc_ref, hidden, wd_ref)
    _residual_steps(f, n_slabs, acc_ref, x_ref, 2.0 * alpha)

    def finish(acc_rows, rows):
        y = _layer_norm(0.5 * acc_ref[acc_rows, :], g_ref[...], b_ref[...])
        o32_ref[rows, :] = y
        o16_ref[rows, :] = y.astype(BF16)

    _norm_steps(f, nf, o32_ref.shape[0], finish)


def _ffn(x, xb, wg, wu, wd, g, b, alpha):
    M, D = x.shape
    F = wg.shape[1]
    tm = _tile(M, 1024, ROWS_BF16_TILE * NORM_SLABS)
    tf = _tile(F, 256)
    nf = F // tf
    n_slabs = min(NORM_SLABS, nf)
    late, early = _slab_spec_factory(tm, D, nf, n_slabs)
    last = nf - 1
    vec = pl.BlockSpec((1, D), lambda i, f: (0, 0))
    return pl.pallas_call(
        functools.partial(_ffn_kernel, alpha=alpha, nf=nf, n_slabs=n_slabs),
        grid=(M // tm, nf + n_slabs),
        in_specs=[
            pl.BlockSpec((tm, D), lambda i, f: (i, 0), pipeline_mode=pl.Buffered(1)),
            pl.BlockSpec((D, tf), lambda i, f: (0, jnp.minimum(f, last))),
            pl.BlockSpec((D, tf), lambda i, f: (0, jnp.minimum(f, last))),
            pl.BlockSpec((tf, D), lambda i, f: (jnp.minimum(f, last), 0)),
            early(), vec, vec,
        ],
        out_specs=[late(), late()],
        out_shape=[jax.ShapeDtypeStruct((M, D), F32), jax.ShapeDtypeStruct((M, D), BF16)],
        scratch_shapes=[pltpu.VMEM((tm, D), F32)],
        compiler_params=_params("parallel", "arbitrary"),
        name="ffn",
    )(xb, wg, wu, wd, x, g, b)


def _glu_proj_kernel(x_ref, wa_ref, wg_ref, ba_ref, bg_ref, o_ref):
    x = x_ref[...]
    a = jnp.dot(x, wa_ref[...], preferred_element_type=F32) + ba_ref[...]
    gt = jnp.dot(x, wg_ref[...], preferred_element_type=F32) + bg_ref[...]
    o_ref[...] = a * _sigmoid(gt)


def _glu_proj(xb, w_in, b_in, d_conv):
    M, D = xb.shape
    tm = _tile(M, 1024, ROWS_BF16_TILE)
    tn = _tile(d_conv, 512)
    off = d_conv // tn
    return pl.pallas_call(
        _glu_proj_kernel,
        grid=(M // tm, d_conv // tn),
        in_specs=[
            pl.BlockSpec((tm, D), lambda i, j: (i, 0)),
            pl.BlockSpec((D, tn), lambda i, j: (0, j)),
            pl.BlockSpec((D, tn), lambda i, j: (0, j + off)),
            pl.BlockSpec((1, tn), lambda i, j: (0, j)),
            pl.BlockSpec((1, tn), lambda i, j: (0, j + off)),
        ],
        out_specs=pl.BlockSpec((tm, tn), lambda i, j: (i, j)),
        out_shape=jax.ShapeDtypeStruct((M, d_conv), F32),
        compiler_params=_params("parallel", "arbitrary"),
        name="glu_proj",
    )(xb, w_in, w_in, b_in, b_in)


def _qkv_proj_kernel(x_ref, w_ref, b_ref, o_ref, *scratch, dilation):
    res = jnp.dot(x_ref[...], w_ref[...], preferred_element_type=F32) + b_ref[...]
    if dilation == 1:
        o_ref[0, 0] = res.astype(o_ref.dtype)
    else:
        (res_ref,) = scratch
        rows = res_ref.shape[1] // dilation
        for c in range(res_ref.shape[0]):
            lanes = slice(c * LANES, (c + 1) * LANES)
            res_ref[c] = res[:, lanes]
            for r in range(dilation):
                o_ref[0, r, :, lanes] = res_ref[c, pl.ds(r, rows, stride=dilation), :].astype(o_ref.dtype)


def _qkv_proj(xb, w_in, b_in, col_q, B, S, group, dilation):
    M, D = xb.shape
    L = S // dilation
    tm = _tile(S, 1024, ROWS_BF16_TILE * dilation)
    tn = _tile(math.gcd(D_GROUP, col_q), 512)
    per_kind = D_GROUP // tn
    tiles_per_seq = S // tm

    def w_col(i, j):
        kind = j // per_kind
        return (0, (col_q + kind * D_ATTN + group * D_GROUP) // tn + j % per_kind)

    return pl.pallas_call(
        functools.partial(_qkv_proj_kernel, dilation=dilation),
        grid=(M // tm, 3 * per_kind),
        in_specs=[
            pl.BlockSpec((tm, D), lambda i, j: (i, 0)),
            pl.BlockSpec((D, tn), w_col),
            pl.BlockSpec((1, tn), w_col),
        ],
        out_specs=pl.BlockSpec((1, dilation, tm // dilation, tn),
                               lambda i, j: (i // tiles_per_seq, 0, i % tiles_per_seq, j)),
        out_shape=jax.ShapeDtypeStruct((B, dilation, L, 3 * D_GROUP), BF16),
        scratch_shapes=[] if dilation == 1 else [pltpu.VMEM((tn // LANES, tm, LANES), F32)],
        compiler_params=_params("parallel", "arbitrary"),
        name=f"qkv_proj_d{dilation}",
    )(xb, w_in, b_in)


def _conv_kernel(prev_ref, cur_ref, next_ref, dw_ref, dwb_ref, g_ref, b_ref, o_ref, buf_ref, shift_ref,
                 *, ts, n_tiles):
    i = pl.program_id(1)
    buf_ref[0:CONV_HALO, :] = jnp.where(i > 0, prev_ref[0], 0.0)
    buf_ref[CONV_HALO:CONV_HALO + ts, :] = cur_ref[0]
    buf_ref[CONV_HALO + ts:, :] = jnp.where(i < n_tiles - 1, next_ref[0], 0.0)
    n_shift = shift_ref.shape[1]
    for s in range(1, SUBLANES_F32):
        shift_ref[s - 1] = buf_ref[s:s + n_shift, :]
    first = CONV_HALO - CONV_WIDTH // 2
    for c in range(ts // CONV_ROWS):
        acc = None
        for w in range(CONV_WIDTH):
            row = first + c * CONV_ROWS + w
            s, base = row % SUBLANES_F32, row - row % SUBLANES_F32
            src = buf_ref if s == 0 else shift_ref.at[s - 1]
            tap = jnp.concatenate([dw_ref[w]] * (CONV_ROWS // SUBLANES_F32), axis=0)
            term = src[base:base + CONV_ROWS, :] * tap
            acc = term if acc is None else acc + term
        y = _layer_norm(acc + dwb_ref[...], g_ref[...], b_ref[...])
        o_ref[0, c * CONV_ROWS:(c + 1) * CONV_ROWS, :] = (y * _sigmoid(y)).astype(o_ref.dtype)


def _conv_module(h, dw, dwb, g, b):
    B, S, C = h.shape
    ts = _tile(S, 128, CONV_HALO)
    n_tiles = S // ts
    hb = ts // CONV_HALO
    n_halo = S // CONV_HALO
    row = lambda bi, i: (0, 0)
    n_buf = ts + 2 * CONV_HALO
    taps = jnp.broadcast_to(dw[:, None, :], (CONV_WIDTH, SUBLANES_F32, C))
    return pl.pallas_call(
        functools.partial(_conv_kernel, ts=ts, n_tiles=n_tiles),
        grid=(B, n_tiles),
        in_specs=[
            pl.BlockSpec((1, CONV_HALO, C), lambda bi, i: (bi, jnp.maximum(i * hb - 1, 0), 0)),
            pl.BlockSpec((1, ts, C), lambda bi, i: (bi, i, 0)),
            pl.BlockSpec((1, CONV_HALO, C), lambda bi, i: (bi, jnp.minimum((i + 1) * hb, n_halo - 1), 0)),
            pl.BlockSpec((CONV_WIDTH, SUBLANES_F32, C), lambda bi, i: (0, 0, 0)),
            pl.BlockSpec((1, C), row),
            pl.BlockSpec((1, C), row),
            pl.BlockSpec((1, C), row),
        ],
        out_specs=pl.BlockSpec((1, ts, C), lambda bi, i: (bi, i, 0)),
        out_shape=jax.ShapeDtypeStruct((B, S, C), BF16),
        scratch_shapes=[pltpu.VMEM((n_buf, C), F32),
                        pltpu.VMEM((SUBLANES_F32 - 1, n_buf - SUBLANES_F32, C), F32)],
        compiler_params=_params("parallel", "arbitrary"),
        name="conv_module",
    )(h, h, h, taps, dwb, g, b)


def _t5_bucket(rel):
    half = NUM_BUCKETS // 2
    max_exact = half // 2
    ret = (rel > 0).astype(np.int32) * half
    n = np.abs(rel)
    large = max_exact + (np.log(np.maximum(n, 1) / max_exact) / np.log(MAX_DISTANCE / max_exact)
                         * (half - max_exact)).astype(np.int32)
    large = np.minimum(large, half - 1)
    return (ret + np.where(n < max_exact, n, large)).astype(np.int32)


def _band_bias(rel_bias, group, dilation):
    qi = np.arange(ATT_TQ)[:, None]
    kj = np.arange(ATT_TQ + 2 * ATT_R)[None, :]
    rel = kj - ATT_R - qi
    band = np.abs(rel) <= ATT_R
    bucket = _t5_bucket(np.clip(rel, -ATT_R, ATT_R) * dilation)
    table = rel_bias[:, group * HEADS_PER_GROUP:(group + 1) * HEADS_PER_GROUP].astype(F32)
    onehot = bucket[None, :, :, None] == np.arange(NUM_BUCKETS)
    bias = jnp.sum(jnp.where(onehot, table.T[:, None, None, :], 0.0), axis=-1)
    return jnp.where(band[None], bias, NEG_INF)


def _attn_kernel(q_ref, kp_ref, kc_ref, kn_ref, vp_ref, vc_ref, vn_ref, bias_ref, o_ref, lse_ref,
                 kbuf_ref, vbuf_ref, *, sub_len, scale):
    i = pl.program_id(2)
    tq = ATT_TQ
    for buf, p, c, n in ((kbuf_ref, kp_ref, kc_ref, kn_ref), (vbuf_ref, vp_ref, vc_ref, vn_ref)):
        buf[0:ATT_R, :] = p[0, 0]
        buf[ATT_R:ATT_R + tq, :] = c[0, 0]
        buf[ATT_R + tq:, :] = n[0, 0]
    key_pos = i * tq - ATT_R + lax.broadcasted_iota(jnp.int32, (1, tq + 2 * ATT_R), 1)
    valid = (key_pos >= 0) & (key_pos < sub_len)
    for h in range(HEADS_PER_GROUP):
        cols = slice(h * HEAD_DIM, (h + 1) * HEAD_DIM)
        s = lax.dot_general(q_ref[0, 0, :, cols], kbuf_ref[:, cols], (((1,), (1,)), ((), ())),
                            preferred_element_type=F32) * scale
        s = jnp.where(valid, s + bias_ref[h], NEG_INF)
        m = jnp.max(s, axis=-1, keepdims=True)
        p = jnp.exp(s - m)
        l = jnp.sum(p, axis=-1, keepdims=True)
        p = p * (1.0 / l)
        o_ref[0, 0, :, cols] = jnp.dot(p.astype(BF16), vbuf_ref[:, cols], preferred_element_type=F32)
        lse_ref[0, 0, :, cols] = jnp.broadcast_to(m + jnp.log(l), (tq, HEAD_DIM))


def _attn_group(qkv, bias):
    B, dilation, L, _ = qkv.shape
    tq = ATT_TQ
    assert L % tq == 0, (L, tq)
    hb = tq // ATT_R
    n_halo = L // ATT_R

    def cur(kind):
        return pl.BlockSpec((1, 1, tq, D_GROUP), lambda b, r, i: (b, r, i, kind))

    def before(kind):
        return pl.BlockSpec((1, 1, ATT_R, D_GROUP), lambda b, r, i: (b, r, jnp.maximum(i * hb - 1, 0), kind))

    def after(kind):
        return pl.BlockSpec((1, 1, ATT_R, D_GROUP),
                            lambda b, r, i: (b, r, jnp.minimum((i + 1) * hb, n_halo - 1), kind))

    out_spec = pl.BlockSpec((1, 1, tq, D_GROUP), lambda b, r, i: (b, r, i, 0))
    out_sds = jax.ShapeDtypeStruct((B, dilation, L, D_GROUP), F32)
    return pl.pallas_call(
        functools.partial(_attn_kernel, sub_len=L, scale=HEAD_DIM ** -0.5),
        grid=(B, dilation, L // tq),
        in_specs=[cur(0), before(1), cur(1), after(1), before(2), cur(2), after(2),
                  pl.BlockSpec((HEADS_PER_GROUP, tq, tq + 2 * ATT_R), lambda b, r, i: (0, 0, 0))],
        out_specs=[out_spec, out_spec],
        out_shape=[out_sds, out_sds],
        scratch_shapes=[pltpu.VMEM((tq + 2 * ATT_R, D_GROUP), BF16), pltpu.VMEM((tq + 2 * ATT_R, D_GROUP), BF16)],
        compiler_params=_params("parallel", "parallel", "arbitrary"),
        name=f"attn_d{dilation}",
    )(qkv, qkv, qkv, qkv, qkv, qkv, qkv, bias)


def _combine_kernel(*refs):
    n = N_GROUPS
    o_refs, l_refs, out_ref, buf_ref = refs[:n], refs[n:2 * n], refs[2 * n], refs[2 * n + 1]

    def token_order(ref, slot, lanes):
        dilation, rows = ref.shape[1], ref.shape[2]
        if dilation == 1:
            return ref[0, 0, :, lanes]
        for r in range(dilation):
            buf_ref[slot, pl.ds(r, rows, stride=dilation), :] = ref[0, r, :, lanes]
        return buf_ref[slot]

    for c in range(out_ref.shape[2] // LANES):
        lanes = slice(c * LANES, (c + 1) * LANES)
        outs = [token_order(ref, g, lanes) for g, ref in enumerate(o_refs)]
        lses = [token_order(ref, n + g, lanes) for g, ref in enumerate(l_refs)]
        mx = functools.reduce(jnp.maximum, lses)
        es = [jnp.exp(l - mx) for l in lses]
        den = functools.reduce(lambda a, b: a + b, es)
        acc = None
        for e, o in zip(es, outs):
            term = (e / den) * o
            acc = term if acc is None else acc + term
        out_ref[0, :, lanes] = acc.astype(out_ref.dtype)


def _combine(outs, lses, S):
    B = outs[0].shape[0]
    N = outs[0].shape[-1]
    max_dil = max(o.shape[1] for o in outs)
    tm = _tile(S, 256, SUBLANES_F32 * max_dil)

    def spec(a):
        d = a.shape[1]
        return pl.BlockSpec((1, d, tm // d, N), lambda b, i: (b, 0, i, 0))

    return pl.pallas_call(
        _combine_kernel,
        grid=(B, S // tm),
        in_specs=[spec(a) for a in (*outs, *lses)],
        out_specs=pl.BlockSpec((1, tm, N), lambda b, i: (b, i, 0)),
        out_shape=jax.ShapeDtypeStruct((B, S, N), BF16),
        scratch_shapes=[pltpu.VMEM((2 * N_GROUPS, tm, LANES), F32)],
        compiler_params=_params("parallel", "arbitrary"),
        name="attn_combine",
    )(*outs, *lses)


def _merge_kernel(x_ref, hc_ref, at_ref, wg0_ref, wg1_ref, wc_ref, wa_ref, b0_ref, b1_ref, o_ref):
    x = x_ref[...]
    g0 = _sigmoid(jnp.dot(x, wg0_ref[...], preferred_element_type=F32) + b0_ref[...])
    g1 = _sigmoid(jnp.dot(x, wg1_ref[...], preferred_element_type=F32) + b1_ref[...])
    conv_out = jnp.dot(hc_ref[...], wc_ref[...], preferred_element_type=F32)
    attn_out = jnp.dot(at_ref[...], wa_ref[...], preferred_element_type=F32)
    o_ref[...] = (g0 * conv_out + g1 * attn_out).astype(o_ref.dtype)


def _merge(xb, hc, attn, w_in, b_in, w_conv_out, w_attn_out, col_gate):
    M, D = xb.shape
    tm = _tile(M, 1024, ROWS_BF16_TILE)
    tn = _tile(math.gcd(D, col_gate), 256)
    off0 = col_gate // tn
    off1 = (col_gate + D) // tn
    return pl.pallas_call(
        _merge_kernel,
        grid=(M // tm, D // tn),
        in_specs=[
            pl.BlockSpec((tm, D), lambda i, j: (i, 0)),
            pl.BlockSpec((tm, hc.shape[1]), lambda i, j: (i, 0)),
            pl.BlockSpec((tm, attn.shape[1]), lambda i, j: (i, 0)),
            pl.BlockSpec((D, tn), lambda i, j: (0, j + off0)),
            pl.BlockSpec((D, tn), lambda i, j: (0, j + off1)),
            pl.BlockSpec((hc.shape[1], tn), lambda i, j: (0, j)),
            pl.BlockSpec((attn.shape[1], tn), lambda i, j: (0, j)),
            pl.BlockSpec((1, tn), lambda i, j: (0, j + off0)),
            pl.BlockSpec((1, tn), lambda i, j: (0, j + off1)),
        ],
        out_specs=pl.BlockSpec((tm, tn), lambda i, j: (i, j)),
        out_shape=jax.ShapeDtypeStruct((M, D), BF16),
        compiler_params=_params("parallel", "arbitrary"),
        name="merge",
    )(xb, hc, attn, w_in, w_in, w_conv_out, w_attn_out, b_in, b_in)


def _out_proj_kernel(a_ref, w_ref, x_ref, g_ref, b_ref, o32_ref, o16_ref, acc_ref, *, alpha, nk, n_slabs):
    k = pl.program_id(1)
    _accumulate_steps(k, nk, acc_ref, lambda: a_ref[...], w_ref)
    _residual_steps(k, n_slabs, acc_ref, x_ref, alpha)

    def finish(acc_rows, rows):
        y = _layer_norm(acc_ref[acc_rows, :], g_ref[...], b_ref[...])
        o32_ref[rows, :] = y
        o16_ref[rows, :] = y.astype(BF16)

    _norm_steps(k, nk, o32_ref.shape[0], finish)


def _out_proj(a, w, x, g, b, alpha):
    M, K = a.shape
    D = w.shape[1]
    tm = _tile(M, 1024, ROWS_BF16_TILE * NORM_SLABS)
    tk = _tile(K, 512)
    nk = K // tk
    n_slabs = min(NORM_SLABS, nk)
    last = nk - 1
    late, early = _slab_spec_factory(tm, D, nk, n_slabs)
    vec = pl.BlockSpec((1, D), lambda i, k: (0, 0))
    return pl.pallas_call(
        functools.partial(_out_proj_kernel, alpha=alpha, nk=nk, n_slabs=n_slabs),
        grid=(M // tm, nk + n_slabs),
        in_specs=[
            pl.BlockSpec((tm, tk), lambda i, k: (i, jnp.minimum(k, last))),
            pl.BlockSpec((tk, D), lambda i, k: (jnp.minimum(k, last), 0)),
            early(), vec, vec,
        ],
        out_specs=[late(), late()],
        out_shape=[jax.ShapeDtypeStruct((M, D), F32), jax.ShapeDtypeStruct((M, D), BF16)],
        scratch_shapes=[pltpu.VMEM((tm, D), F32)],
        compiler_params=_params("parallel", "arbitrary"),
        name="out_proj",
    )(a, w, x, g, b)


def _ple_kernel(a_ref, w_ref, x_ref, p_ref, wp_ref, bg_ref, g_ref, b_ref, o_ref, acc_ref, pw_ref, *, alpha, nk):
    k = pl.program_id(1)
    _accumulate_steps(k, nk, acc_ref, lambda: a_ref[...], w_ref)

    @pl.when(k >= nk)
    def _():
        _dot_into(pw_ref, p_ref[...].astype(BF16), wp_ref, accumulate=False)

    def finish(acc_rows, rows):
        gate = _sigmoid(acc_ref[acc_rows, :] + bg_ref[...])
        o_ref[rows, :] = _layer_norm(alpha * x_ref[rows, :] + pw_ref[rows, :] * gate, g_ref[...], b_ref[...])

    _norm_steps(k, nk, o_ref.shape[0], finish)


def _ple(a, w, x, p, wp, bg, g, b, alpha):
    M, K = a.shape
    D = w.shape[1]
    P = p.shape[1]
    tm = _tile(M, 1024, ROWS_BF16_TILE * NORM_SLABS)
    tk = _tile(K, 1024)
    nk = K // tk
    last = nk - 1
    n_slabs = NORM_SLABS
    late, _ = _slab_spec_factory(tm, D, nk, n_slabs)
    vec = pl.BlockSpec((1, D), lambda i, k: (0, 0))
    return pl.pallas_call(
        functools.partial(_ple_kernel, alpha=alpha, nk=nk),
        grid=(M // tm, nk + n_slabs),
        in_specs=[
            pl.BlockSpec((tm, tk), lambda i, k: (i, jnp.minimum(k, last))),
            pl.BlockSpec((tk, D), lambda i, k: (jnp.minimum(k, last), 0)),
            late(), late(P),
            pl.BlockSpec((P, D), lambda i, k: (0, 0)),
            vec, vec, vec,
        ],
        out_specs=late(),
        out_shape=jax.ShapeDtypeStruct((M, D), F32),
        scratch_shapes=[pltpu.VMEM((tm, D), F32), pltpu.VMEM((tm // n_slabs, D), F32)],
        compiler_params=_params("parallel", "arbitrary"),
        name="ple",
    )(a, w, x, p, wp, bg, g, b)


def _encoder_layer(x, p, B, S, w, biases, alpha):
    d_conv = w["conv_dw"].shape[1]
    col_q = 2 * d_conv
    col_gate = col_q + 3 * D_ATTN
    ln_g, ln_b = w["ln_g"], w["ln_b"]

    x1, x1b = _ffn(x, x.astype(BF16), w["w_ff1_gate"], w["w_ff1_up"], w["w_ff1_down"], ln_g[0:1], ln_b[0:1], alpha)

    h = _glu_proj(x1b, w["w_in"], w["b_in"], d_conv)
    hc = _conv_module(h.reshape(B, S, d_conv), w["conv_dw"], w["conv_dw_b"], w["conv_ln_g"], w["conv_ln_b"])
    hc = hc.reshape(B * S, d_conv)

    outs, lses = [], []
    for gi, (_, dil) in enumerate(DILATED_GROUPS):
        qkv = _qkv_proj(x1b, w["w_in"], w["b_in"], col_q, B, S, gi, dil)
        o, lse = _attn_group(qkv, biases[gi])
        outs.append(o)
        lses.append(lse)
    attn = _combine(outs, lses, S).reshape(B * S, D_GROUP)

    merged = _merge(x1b, hc, attn, w["w_in"], w["b_in"], w["w_conv_out"], w["w_attn_out"], col_gate)
    x2, x2b = _out_proj(merged, w["w_out"], x1, ln_g[1:2], ln_b[1:2], alpha)
    x3, x3b = _ffn(x2, x2b, w["w_ff2_gate"], w["w_ff2_up"], w["w_ff2_down"], ln_g[2:3], ln_b[2:3], alpha)
    return _ple(x3b, w["w_ple_gate"], x3, p, w["w_ple"], w["b_ple_gate"], ln_g[3:4], ln_b[3:4], alpha)


_MATMUL_WEIGHTS = ("w_ff1_gate", "w_ff1_up", "w_ff1_down", "w_in", "w_conv_out", "w_attn_out", "w_out",
                   "w_ff2_gate", "w_ff2_up", "w_ff2_down", "w_ple", "w_ple_gate")
_ROW_VECTORS = ("b_in", "conv_dw_b", "conv_ln_g", "conv_ln_b", "b_ple_gate")


def kernel(x_prompt, x_sample, p_prompt, p_sample, rel_bias, ln_g, ln_b, w_ff1_gate, w_ff1_up, w_ff1_down,
           w_in, b_in, conv_dw, conv_dw_b, conv_ln_g, conv_ln_b, w_conv_out, w_attn_out, w_out, w_ff2_gate,
           w_ff2_up, w_ff2_down, w_ple, w_ple_gate, b_ple_gate):
    stacked = dict(ln_g=ln_g, ln_b=ln_b, w_ff1_gate=w_ff1_gate, w_ff1_up=w_ff1_up, w_ff1_down=w_ff1_down,
                   w_in=w_in, b_in=b_in, conv_dw=conv_dw, conv_dw_b=conv_dw_b, conv_ln_g=conv_ln_g,
                   conv_ln_b=conv_ln_b, w_conv_out=w_conv_out, w_attn_out=w_attn_out, w_out=w_out,
                   w_ff2_gate=w_ff2_gate, w_ff2_up=w_ff2_up, w_ff2_down=w_ff2_down, w_ple=w_ple,
                   w_ple_gate=w_ple_gate, b_ple_gate=b_ple_gate)
    depth = ln_g.shape[0]
    alpha = (2.0 * depth) ** 0.25
    layers = []
    for i in range(depth):
        w = {k: v[i] for k, v in stacked.items()}
        for k in _MATMUL_WEIGHTS:
            w[k] = w[k].astype(BF16)
        for k in _ROW_VECTORS:
            w[k] = w[k].reshape(1, -1)
        layers.append(w)
    biases = [_band_bias(rel_bias, gi, dil) for gi, (_, dil) in enumerate(DILATED_GROUPS)]

    def run(x, p):
        B, S, D = x.shape
        y = x.reshape(B * S, D)
        for i, w in enumerate(layers):
            y = _encoder_layer(y, p[i].reshape(B * S, -1), B, S, w, biases, alpha)
        return y.reshape(B, S, D)

    return run(x_prompt, p_prompt), run(x_sample, p_sample)
```

```python
import functools
import math

import jax
import jax.numpy as jnp
import numpy as np
from jax import lax
from jax.experimental import pallas as pl
from jax.experimental.pallas import tpu as pltpu

F32 = jnp.float32
BF16 = jnp.bfloat16

HEAD_DIM = 128
HEADS_PER_GROUP = 8
DILATED_GROUPS = ((128, 1), (512, 4), (2048, 16))
N_GROUPS = len(DILATED_GROUPS)
D_GROUP = HEADS_PER_GROUP * HEAD_DIM
D_ATTN = N_GROUPS * D_GROUP
CONV_WIDTH = 31
NUM_BUCKETS = 32
MAX_DISTANCE = 1024
LN_EPS = 1e-5
NEG_INF = -1e30

LANES = 128
SUBLANES_F32 = 8
ROWS_BF16_TILE = 16
VMEM_LIMIT_BYTES = 58 * 1024 * 1024
FFN_VMEM_LIMIT_BYTES = 63 * 1024 * 1024

ATT_R = 64
ATT_TQ = 128
CONV_HALO = 16
CONV_ROWS = 16
ACC_CHUNK_COLS = 512
EPILOGUE_ROWS = 64
NORM_SLABS = 8
FFN_NORM_SLABS = 16


def _tile(n, target, align=LANES):
    if n <= target:
        return n
    t = (target // align) * align
    while t >= align:
        if n % t == 0:
            return t
        t -= align
    raise ValueError(f"no tile for {n} <= {target} aligned to {align}")


def _params(*sem, vmem_limit_bytes=VMEM_LIMIT_BYTES):
    return pltpu.CompilerParams(dimension_semantics=sem, vmem_limit_bytes=vmem_limit_bytes)


def _layer_norm(r, g, b):
    mu = jnp.mean(r, axis=-1, keepdims=True)
    c = r - mu
    var = jnp.mean(c * c, axis=-1, keepdims=True)
    return c * lax.rsqrt(var + LN_EPS) * g + b


def _sigmoid(x):
    return 1.0 / (1.0 + jnp.exp(-x))


def _for_row_chunks(n_rows, fn):
    def body(c, carry):
        fn(pl.multiple_of(c * EPILOGUE_ROWS, EPILOGUE_ROWS))
        return carry
    lax.fori_loop(0, n_rows // EPILOGUE_ROWS, body, 0)


def _dot_into(acc_ref, terms, accumulate):
    n = acc_ref.shape[1]
    tn = _tile(n, ACC_CHUNK_COLS)
    for c in range(n // tn):
        cols = slice(c * tn, (c + 1) * tn)
        prod = None
        for a, w_ref in terms:
            term = jnp.dot(a, w_ref[:, cols], preferred_element_type=F32)
            prod = term if prod is None else prod + term
        acc_ref[:, cols] = acc_ref[:, cols] + prod if accumulate else prod


def _accumulate_steps(step, n_main, acc_ref, operand_fn, w_ref):
    @pl.when(step == 0)
    def _():
        _dot_into(acc_ref, [(operand_fn(), w_ref)], accumulate=False)

    @pl.when((step > 0) & (step < n_main))
    def _():
        _dot_into(acc_ref, [(operand_fn(), w_ref)], accumulate=True)


def _residual_steps(step, n_slabs, acc_ref, x_ref, scale):
    slab_rows = x_ref.shape[0]

    @pl.when(step < n_slabs)
    def _():
        slab0 = step * slab_rows

        def add(r):
            rows = pl.ds(pl.multiple_of(slab0 + r, EPILOGUE_ROWS), EPILOGUE_ROWS)
            acc_ref[rows, :] = acc_ref[rows, :] + scale * x_ref[pl.ds(r, EPILOGUE_ROWS), :]
        _for_row_chunks(slab_rows, add)


def _norm_steps(step, n_main, slab_rows, fn):
    @pl.when(step >= n_main)
    def _():
        slab0 = (step - n_main) * slab_rows
        _for_row_chunks(slab_rows, lambda r: fn(pl.ds(pl.multiple_of(slab0 + r, EPILOGUE_ROWS), EPILOGUE_ROWS),
                                                pl.ds(r, EPILOGUE_ROWS)))


def _slab_spec_factory(tm, width, n_main, n_slabs):
    rows = tm // n_slabs

    def late(cols=width):
        return pl.BlockSpec((rows, cols), lambda i, s: (i * n_slabs + jnp.clip(s - n_main, 0, n_slabs - 1), 0))

    def early(cols=width):
        return pl.BlockSpec((rows, cols), lambda i, s: (i * n_slabs + jnp.minimum(s, n_slabs - 1), 0))
    return late, early


def _ffn_kernel(xb_ref, wg0_ref, wg1_ref, wu0_ref, wu1_ref, wd0_ref, wd1_ref, x_ref, g_ref, b_ref,
                o32_ref, o16_ref, acc_ref, *, alpha, nf, n_slabs):
    s = pl.program_id(1)
    n_main = (nf + 1) // 2

    def terms(n_chunks):
        xb = xb_ref[...]
        out = []
        for wg_ref, wu_ref, wd_ref in ((wg0_ref, wu0_ref, wd0_ref), (wg1_ref, wu1_ref, wd1_ref))[:n_chunks]:
            gt = jnp.dot(xb, wg_ref[...], preferred_element_type=F32)
            up = jnp.dot(xb, wu_ref[...], preferred_element_type=F32)
            out.append((((gt * _sigmoid(gt)) * up).astype(BF16), wd_ref))
        return out

    n_full = nf // 2

    @pl.when(s == 0)
    def _():
        _dot_into(acc_ref, terms(2 if n_full > 0 else 1), accumulate=False)

    @pl.when((s > 0) & (s < n_full))
    def _():
        _dot_into(acc_ref, terms(2), accumulate=True)

    if nf % 2 and n_full > 0:
        @pl.when(s == n_full)
        def _():
            _dot_into(acc_ref, terms(1), accumulate=True)

    _residual_steps(s, n_slabs, acc_ref, x_ref, 2.0 * alpha)

    def finish(acc_rows, rows):
        y = _layer_norm(0.5 * acc_ref[acc_rows, :], g_ref[...], b_ref[...])
        o32_ref[rows, :] = y
        o16_ref[rows, :] = y.astype(BF16)

    _norm_steps(s, n_main, o32_ref.shape[0], finish)


def _ffn(x, xb, wg, wu, wd, g, b, alpha):
    M, D = x.shape
    F = wg.shape[1]
    tf = _tile(F, 256)
    nf = F // tf
    n_main = (nf + 1) // 2
    n_slabs = min(FFN_NORM_SLABS, n_main)
    tm = _tile(M, 1024, EPILOGUE_ROWS * n_slabs)
    late, early = _slab_spec_factory(tm, D, n_main, n_slabs)
    last = nf - 1
    vec = pl.BlockSpec((1, D), lambda i, s: (0, 0))

    def cols(which):
        return pl.BlockSpec((D, tf), lambda i, s: (0, jnp.minimum(2 * s + which, last)))

    def rows(which):
        return pl.BlockSpec((tf, D), lambda i, s: (jnp.minimum(2 * s + which, last), 0))

    return pl.pallas_call(
        functools.partial(_ffn_kernel, alpha=alpha, nf=nf, n_slabs=n_slabs),
        grid=(M // tm, n_main + n_slabs),
        in_specs=[
            pl.BlockSpec((tm, D), lambda i, s: (i, 0), pipeline_mode=pl.Buffered(1)),
            cols(0), cols(1), cols(0), cols(1), rows(0), rows(1),
            early(), vec, vec,
        ],
        out_specs=[late(), late()],
        out_shape=[jax.ShapeDtypeStruct((M, D), F32), jax.ShapeDtypeStruct((M, D), BF16)],
        scratch_shapes=[pltpu.VMEM((tm, D), F32)],
        compiler_params=_params("parallel", "arbitrary", vmem_limit_bytes=FFN_VMEM_LIMIT_BYTES),
        name="ffn",
    )(xb, wg, wg, wu, wu, wd, wd, x, g, b)


def _glu_proj_kernel(x_ref, wa_ref, wg_ref, ba_ref, bg_ref, o_ref):
    x = x_ref[...]
    a = jnp.dot(x, wa_ref[...], preferred_element_type=F32) + ba_ref[...]
    gt = jnp.dot(x, wg_ref[...], preferred_element_type=F32) + bg_ref[...]
    o_ref[...] = a * _sigmoid(gt)


def _glu_proj(xb, w_in, b_in, d_conv):
    M, D = xb.shape
    tm = _tile(M, 1024, ROWS_BF16_TILE)
    tn = _tile(d_conv, 512)
    off = d_conv // tn
    return pl.pallas_call(
        _glu_proj_kernel,
        grid=(M // tm, d_conv // tn),
        in_specs=[
            pl.BlockSpec((tm, D), lambda i, j: (i, 0)),
            pl.BlockSpec((D, tn), lambda i, j: (0, j)),
            pl.BlockSpec((D, tn), lambda i, j: (0, j + off)),
            pl.BlockSpec((1, tn), lambda i, j: (0, j)),
            pl.BlockSpec((1, tn), lambda i, j: (0, j + off)),
        ],
        out_specs=pl.BlockSpec((tm, tn), lambda i, j: (i, j)),
        out_shape=jax.ShapeDtypeStruct((M, d_conv), F32),
        compiler_params=_params("parallel", "arbitrary"),
        name="glu_proj",
    )(xb, w_in, w_in, b_in, b_in)


def _qkv_proj_kernel(x_ref, w_ref, b_ref, o_ref, *scratch, dilation):
    res = jnp.dot(x_ref[...], w_ref[...], preferred_element_type=F32) + b_ref[...]
    if dilation == 1:
        o_ref[0, 0] = res.astype(o_ref.dtype)
    else:
        (res_ref,) = scratch
        rows = res_ref.shape[1] // dilation
        for c in range(res_ref.shape[0]):
            lanes = slice(c * LANES, (c + 1) * LANES)
            res_ref[c] = res[:, lanes]
            for r in range(dilation):
                o_ref[0, r, :, lanes] = res_ref[c, pl.ds(r, rows, stride=dilation), :].astype(o_ref.dtype)


def _qkv_proj(xb, w_in, b_in, col_q, B, S, group, dilation):
    M, D = xb.shape
    L = S // dilation
    tm = _tile(S, 1024, ROWS_BF16_TILE * dilation)
    tn = _tile(math.gcd(D_GROUP, col_q), 512)
    per_kind = D_GROUP // tn
    tiles_per_seq = S // tm

    def w_col(i, j):
        kind = j // per_kind
        return (0, (col_q + kind * D_ATTN + group * D_GROUP) // tn + j % per_kind)

    return pl.pallas_call(
        functools.partial(_qkv_proj_kernel, dilation=dilation),
        grid=(M // tm, 3 * per_kind),
        in_specs=[
            pl.BlockSpec((tm, D), lambda i, j: (i, 0)),
            pl.BlockSpec((D, tn), w_col),
            pl.BlockSpec((1, tn), w_col),
        ],
        out_specs=pl.BlockSpec((1, dilation, tm // dilation, tn),
                               lambda i, j: (i // tiles_per_seq, 0, i % tiles_per_seq, j)),
        out_shape=jax.ShapeDtypeStruct((B, dilation, L, 3 * D_GROUP), BF16),
        scratch_shapes=[] if dilation == 1 else [pltpu.VMEM((tn // LANES, tm, LANES), F32)],
        compiler_params=_params("parallel", "arbitrary"),
        name=f"qkv_proj_d{dilation}",
    )(xb, w_in, b_in)


def _conv_kernel(prev_ref, cur_ref, next_ref, dw_ref, dwb_ref, g_ref, b_ref, o_ref, buf_ref, shift_ref,
                 *, ts, n_tiles):
    i = pl.program_id(1)
    buf_ref[0:CONV_HALO, :] = jnp.where(i > 0, prev_ref[0], 0.0)
    buf_ref[CONV_HALO:CONV_HALO + ts, :] = cur_ref[0]
    buf_ref[CONV_HALO + ts:, :] = jnp.where(i < n_tiles - 1, next_ref[0], 0.0)
    n_shift = shift_ref.shape[1]
    for s in range(1, SUBLANES_F32):
        shift_ref[s - 1] = buf_ref[s:s + n_shift, :]
    first = CONV_HALO - CONV_WIDTH // 2
    for c in range(ts // CONV_ROWS):
        acc = None
        for w in range(CONV_WIDTH):
            row = first + c * CONV_ROWS + w
            s, base = row % SUBLANES_F32, row - row % SUBLANES_F32
            src = buf_ref if s == 0 else shift_ref.at[s - 1]
            tap = jnp.concatenate([dw_ref[w]] * (CONV_ROWS // SUBLANES_F32), axis=0)
            term = src[base:base + CONV_ROWS, :] * tap
            acc = term if acc is None else acc + term
        y = _layer_norm(acc + dwb_ref[...], g_ref[...], b_ref[...])
        o_ref[0, c * CONV_ROWS:(c + 1) * CONV_ROWS, :] = (y * _sigmoid(y)).astype(o_ref.dtype)


def _conv_module(h, dw, dwb, g, b):
    B, S, C = h.shape
    ts = _tile(S, 128, CONV_HALO)
    n_tiles = S // ts
    hb = ts // CONV_HALO
    n_halo = S // CONV_HALO
    row = lambda bi, i: (0, 0)
    n_buf = ts + 2 * CONV_HALO
    taps = jnp.broadcast_to(dw[:, None, :], (CONV_WIDTH, SUBLANES_F32, C))
    return pl.pallas_call(
        functools.partial(_conv_kernel, ts=ts, n_tiles=n_tiles),
        grid=(B, n_tiles),
        in_specs=[
            pl.BlockSpec((1, CONV_HALO, C), lambda bi, i: (bi, jnp.maximum(i * hb - 1, 0), 0)),
            pl.BlockSpec((1, ts, C), lambda bi, i: (bi, i, 0)),
            pl.BlockSpec((1, CONV_HALO, C), lambda bi, i: (bi, jnp.minimum((i + 1) * hb, n_halo - 1), 0)),
            pl.BlockSpec((CONV_WIDTH, SUBLANES_F32, C), lambda bi, i: (0, 0, 0)),
            pl.BlockSpec((1, C), row),
            pl.BlockSpec((1, C), row),
            pl.BlockSpec((1, C), row),
        ],
        out_specs=pl.BlockSpec((1, ts, C), lambda bi, i: (bi, i, 0)),
        out_shape=jax.ShapeDtypeStruct((B, S, C), BF16),
        scratch_shapes=[pltpu.VMEM((n_buf, C), F32),
                        pltpu.VMEM((SUBLANES_F32 - 1, n_buf - SUBLANES_F32, C), F32)],
        compiler_params=_params("parallel", "arbitrary"),
        name="conv_module",
    )(h, h, h, taps, dwb, g, b)


def _t5_bucket(rel):
    half = NUM_BUCKETS // 2
    max_exact = half // 2
    ret = (rel > 0).astype(np.int32) * half
    n = np.abs(rel)
    large = max_exact + (np.log(np.maximum(n, 1) / max_exact) / np.log(MAX_DISTANCE / max_exact)
                         * (half - max_exact)).astype(np.int32)
    large = np.minimum(large, half - 1)
    return (ret + np.where(n < max_exact, n, large)).astype(np.int32)


def _band_bias(rel_bias, group, dilation):
    qi = np.arange(ATT_TQ)[:, None]
    kj = np.arange(ATT_TQ + 2 * ATT_R)[None, :]
    rel = kj - ATT_R - qi
    band = np.abs(rel) <= ATT_R
    bucket = _t5_bucket(np.clip(rel, -ATT_R, ATT_R) * dilation)
    table = rel_bias[:, group * HEADS_PER_GROUP:(group + 1) * HEADS_PER_GROUP].astype(F32)
    onehot = bucket[None, :, :, None] == np.arange(NUM_BUCKETS)
    bias = jnp.sum(jnp.where(onehot, table.T[:, None, None, :], 0.0), axis=-1)
    return jnp.where(band[None], bias, NEG_INF)


def _attn_kernel(q_ref, kp_ref, kc_ref, kn_ref, vp_ref, vc_ref, vn_ref, bias_ref, o_ref, lse_ref,
                 kbuf_ref, vbuf_ref, *, sub_len, scale):
    i = pl.program_id(2)
    tq = ATT_TQ
    for buf, p, c, n in ((kbuf_ref, kp_ref, kc_ref, kn_ref), (vbuf_ref, vp_ref, vc_ref, vn_ref)):
        buf[0:ATT_R, :] = p[0, 0]
        buf[ATT_R:ATT_R + tq, :] = c[0, 0]
        buf[ATT_R + tq:, :] = n[0, 0]
    key_pos = i * tq - ATT_R + lax.broadcasted_iota(jnp.int32, (1, tq + 2 * ATT_R), 1)
    valid = (key_pos >= 0) & (key_pos < sub_len)
    for h in range(HEADS_PER_GROUP):
        cols = slice(h * HEAD_DIM, (h + 1) * HEAD_DIM)
        s = lax.dot_general(q_ref[0, 0, :, cols], kbuf_ref[:, cols], (((1,), (1,)), ((), ())),
                            preferred_element_type=F32) * scale
        s = jnp.where(valid, s + bias_ref[h], NEG_INF)
        m = jnp.max(s, axis=-1, keepdims=True)
        p = jnp.exp(s - m)
        l = jnp.sum(p, axis=-1, keepdims=True)
        p = p * (1.0 / l)
        o_ref[0, 0, :, cols] = jnp.dot(p.astype(BF16), vbuf_ref[:, cols], preferred_element_type=F32)
        lse_ref[0, 0, :, cols] = jnp.broadcast_to(m + jnp.log(l), (tq, HEAD_DIM))


def _attn_group(qkv, bias):
    B, dilation, L, _ = qkv.shape
    tq = ATT_TQ
    assert L % tq == 0, (L, tq)
    hb = tq // ATT_R
    n_halo = L // ATT_R

    def cur(kind):
        return pl.BlockSpec((1, 1, tq, D_GROUP), lambda b, r, i: (b, r, i, kind))

    def before(kind):
        return pl.BlockSpec((1, 1, ATT_R, D_GROUP), lambda b, r, i: (b, r, jnp.maximum(i * hb - 1, 0), kind))

    def after(kind):
        return pl.BlockSpec((1, 1, ATT_R, D_GROUP),
                            lambda b, r, i: (b, r, jnp.minimum((i + 1) * hb, n_halo - 1), kind))

    out_spec = pl.BlockSpec((1, 1, tq, D_GROUP), lambda b, r, i: (b, r, i, 0))
    out_sds = jax.ShapeDtypeStruct((B, dilation, L, D_GROUP), F32)
    return pl.pallas_call(
        functools.partial(_attn_kernel, sub_len=L, scale=HEAD_DIM ** -0.5),
        grid=(B, dilation, L // tq),
        in_specs=[cur(0), before(1), cur(1), after(1), before(2), cur(2), after(2),
                  pl.BlockSpec((HEADS_PER_GROUP, tq, tq + 2 * ATT_R), lambda b, r, i: (0, 0, 0))],
        out_specs=[out_spec, out_spec],
        out_shape=[out_sds, out_sds],
        scratch_shapes=[pltpu.VMEM((tq + 2 * ATT_R, D_GROUP), BF16), pltpu.VMEM((tq + 2 * ATT_R, D_GROUP), BF16)],
        compiler_params=_params("parallel", "parallel", "arbitrary"),
        name=f"attn_d{dilation}",
    )(qkv, qkv, qkv, qkv, qkv, qkv, qkv, bias)


def _combine_kernel(*refs):
    n = N_GROUPS
    o_refs, l_refs, out_ref, buf_ref = refs[:n], refs[n:2 * n], refs[2 * n], refs[2 * n + 1]

    def token_order(ref, slot, lanes):
        dilation, rows = ref.shape[1], ref.shape[2]
        if dilation == 1:
            return ref[0, 0, :, lanes]
        for r in range(dilation):
            buf_ref[slot, pl.ds(r, rows, stride=dilation), :] = ref[0, r, :, lanes]
        return buf_ref[slot]

    for c in range(out_ref.shape[2] // LANES):
        lanes = slice(c * LANES, (c + 1) * LANES)
        outs = [token_order(ref, g, lanes) for g, ref in enumerate(o_refs)]
        lses = [token_order(ref, n + g, lanes) for g, ref in enumerate(l_refs)]
        mx = functools.reduce(jnp.maximum, lses)
        es = [jnp.exp(l - mx) for l in lses]
        den = functools.reduce(lambda a, b: a + b, es)
        acc = None
        for e, o in zip(es, outs):
            term = (e / den) * o
            acc = term if acc is None else acc + term
        out_ref[0, :, lanes] = acc.astype(out_ref.dtype)


def _combine(outs, lses, S):
    B = outs[0].shape[0]
    N = outs[0].shape[-1]
    max_dil = max(o.shape[1] for o in outs)
    tm = _tile(S, 256, SUBLANES_F32 * max_dil)

    def spec(a):
        d = a.shape[1]
        return pl.BlockSpec((1, d, tm // d, N), lambda b, i: (b, 0, i, 0))

    return pl.pallas_call(
        _combine_kernel,
        grid=(B, S // tm),
        in_specs=[spec(a) for a in (*outs, *lses)],
        out_specs=pl.BlockSpec((1, tm, N), lambda b, i: (b, i, 0)),
        out_shape=jax.ShapeDtypeStruct((B, S, N), BF16),
        scratch_shapes=[pltpu.VMEM((2 * N_GROUPS, tm, LANES), F32)],
        compiler_params=_params("parallel", "arbitrary"),
        name="attn_combine",
    )(*outs, *lses)


def _merge_kernel(x_ref, hc_ref, at_ref, wg0_ref, wg1_ref, wc_ref, wa_ref, b0_ref, b1_ref, o_ref):
    x = x_ref[...]
    g0 = _sigmoid(jnp.dot(x, wg0_ref[...], preferred_element_type=F32) + b0_ref[...])
    g1 = _sigmoid(jnp.dot(x, wg1_ref[...], preferred_element_type=F32) + b1_ref[...])
    conv_out = jnp.dot(hc_ref[...], wc_ref[...], preferred_element_type=F32)
    attn_out = jnp.dot(at_ref[...], wa_ref[...], preferred_element_type=F32)
    o_ref[...] = (g0 * conv_out + g1 * attn_out).astype(o_ref.dtype)


def _merge(xb, hc, attn, w_in, b_in, w_conv_out, w_attn_out, col_gate):
    M, D = xb.shape
    tm = _tile(M, 1024, ROWS_BF16_TILE)
    tn = _tile(math.gcd(D, col_gate), 256)
    off0 = col_gate // tn
    off1 = (col_gate + D) // tn
    return pl.pallas_call(
        _merge_kernel,
        grid=(M // tm, D // tn),
        in_specs=[
            pl.BlockSpec((tm, D), lambda i, j: (i, 0)),
            pl.BlockSpec((tm, hc.shape[1]), lambda i, j: (i, 0)),
            pl.BlockSpec((tm, attn.shape[1]), lambda i, j: (i, 0)),
            pl.BlockSpec((D, tn), lambda i, j: (0, j + off0)),
            pl.BlockSpec((D, tn), lambda i, j: (0, j + off1)),
            pl.BlockSpec((hc.shape[1], tn), lambda i, j: (0, j)),
            pl.BlockSpec((attn.shape[1], tn), lambda i, j: (0, j)),
            pl.BlockSpec((1, tn), lambda i, j: (0, j + off0)),
            pl.BlockSpec((1, tn), lambda i, j: (0, j + off1)),
        ],
        out_specs=pl.BlockSpec((tm, tn), lambda i, j: (i, j)),
        out_shape=jax.ShapeDtypeStruct((M, D), BF16),
        compiler_params=_params("parallel", "arbitrary"),
        name="merge",
    )(xb, hc, attn, w_in, w_in, w_conv_out, w_attn_out, b_in, b_in)


def _out_proj_kernel(a_ref, w_ref, x_ref, g_ref, b_ref, o32_ref, o16_ref, acc_ref, *, alpha, nk, n_slabs):
    k = pl.program_id(1)
    _accumulate_steps(k, nk, acc_ref, lambda: a_ref[...], w_ref)
    _residual_steps(k, n_slabs, acc_ref, x_ref, alpha)

    def finish(acc_rows, rows):
        y = _layer_norm(acc_ref[acc_rows, :], g_ref[...], b_ref[...])
        o32_ref[rows, :] = y
        o16_ref[rows, :] = y.astype(BF16)

    _norm_steps(k, nk, o32_ref.shape[0], finish)


def _out_proj(a, w, x, g, b, alpha):
    M, K = a.shape
    D = w.shape[1]
    tm = _tile(M, 1024, ROWS_BF16_TILE * NORM_SLABS)
    tk = _tile(K, 512)
    nk = K // tk
    n_slabs = min(NORM_SLABS, nk)
    last = nk - 1
    late, early = _slab_spec_factory(tm, D, nk, n_slabs)
    vec = pl.BlockSpec((1, D), lambda i, k: (0, 0))
    return pl.pallas_call(
        functools.partial(_out_proj_kernel, alpha=alpha, nk=nk, n_slabs=n_slabs),
        grid=(M // tm, nk + n_slabs),
        in_specs=[
            pl.BlockSpec((tm, tk), lambda i, k: (i, jnp.minimum(k, last))),
            pl.BlockSpec((tk, D), lambda i, k: (jnp.minimum(k, last), 0)),
            early(), vec, vec,
        ],
        out_specs=[late(), late()],
        out_shape=[jax.ShapeDtypeStruct((M, D), F32), jax.ShapeDtypeStruct((M, D), BF16)],
        scratch_shapes=[pltpu.VMEM((tm, D), F32)],
        compiler_params=_params("parallel", "arbitrary"),
        name="out_proj",
    )(a, w, x, g, b)


def _ple_kernel(a_ref, w_ref, x_ref, p_ref, wp_ref, bg_ref, g_ref, b_ref, o_ref, acc_ref, pw_ref, *, alpha, nk):
    k = pl.program_id(1)
    _accumulate_steps(k, nk, acc_ref, lambda: a_ref[...], w_ref)

    @pl.when(k >= nk)
    def _():
        _dot_into(pw_ref, [(p_ref[...].astype(BF16), wp_ref)], accumulate=False)

    def finish(acc_rows, rows):
        gate = _sigmoid(acc_ref[acc_rows, :] + bg_ref[...])
        o_ref[rows, :] = _layer_norm(alpha * x_ref[rows, :] + pw_ref[rows, :] * gate, g_ref[...], b_ref[...])

    _norm_steps(k, nk, o_ref.shape[0], finish)


def _ple(a, w, x, p, wp, bg, g, b, alpha):
    M, K = a.shape
    D = w.shape[1]
    P = p.shape[1]
    tm = _tile(M, 1024, ROWS_BF16_TILE * NORM_SLABS)
    tk = _tile(K, 1024)
    nk = K // tk
    last = nk - 1
    n_slabs = NORM_SLABS
    late, _ = _slab_spec_factory(tm, D, nk, n_slabs)
    vec = pl.BlockSpec((1, D), lambda i, k: (0, 0))
    return pl.pallas_call(
        functools.partial(_ple_kernel, alpha=alpha, nk=nk),
        grid=(M // tm, nk + n_slabs),
        in_specs=[
            pl.BlockSpec((tm, tk), lambda i, k: (i, jnp.minimum(k, last))),
            pl.BlockSpec((tk, D), lambda i, k: (jnp.minimum(k, last), 0)),
            late(), late(P),
            pl.BlockSpec((P, D), lambda i, k: (0, 0)),
            vec, vec, vec,
        ],
        out_specs=late(),
        out_shape=jax.ShapeDtypeStruct((M, D), F32),
        scratch_shapes=[pltpu.VMEM((tm, D), F32), pltpu.VMEM((tm // n_slabs, D), F32)],
        compiler_params=_params("parallel", "arbitrary"),
        name="ple",
    )(a, w, x, p, wp, bg, g, b)


def _encoder_layer(x, p, B, S, w, biases, alpha):
    d_conv = w["conv_dw"].shape[1]
    col_q = 2 * d_conv
    col_gate = col_q + 3 * D_ATTN
    ln_g, ln_b = w["ln_g"], w["ln_b"]

    x1, x1b = _ffn(x, x.astype(BF16), w["w_ff1_gate"], w["w_ff1_up"], w["w_ff1_down"], ln_g[0:1], ln_b[0:1], alpha)

    h = _glu_proj(x1b, w["w_in"], w["b_in"], d_conv)
    hc = _conv_module(h.reshape(B, S, d_conv), w["conv_dw"], w["conv_dw_b"], w["conv_ln_g"], w["conv_ln_b"])
    hc = hc.reshape(B * S, d_conv)

    outs, lses = [], []
    for gi, (_, dil) in enumerate(DILATED_GROUPS):
        qkv = _qkv_proj(x1b, w["w_in"], w["b_in"], col_q, B, S, gi, dil)
        o, lse = _attn_group(qkv, biases[gi])
        outs.append(o)
        lses.append(lse)
    attn = _combine(outs, lses, S).reshape(B * S, D_GROUP)

    merged = _merge(x1b, hc, attn, w["w_in"], w["b_in"], w["w_conv_out"], w["w_attn_out"], col_gate)
    x2, x2b = _out_proj(merged, w["w_out"], x1, ln_g[1:2], ln_b[1:2], alpha)
    x3, x3b = _ffn(x2, x2b, w["w_ff2_gate"], w["w_ff2_up"], w["w_ff2_down"], ln_g[2:3], ln_b[2:3], alpha)
    return _ple(x3b, w["w_ple_gate"], x3, p, w["w_ple"], w["b_ple_gate"], ln_g[3:4], ln_b[3:4], alpha)


_MATMUL_WEIGHTS = ("w_ff1_gate", "w_ff1_up", "w_ff1_down", "w_in", "w_conv_out", "w_attn_out", "w_out",
                   "w_ff2_gate", "w_ff2_up", "w_ff2_down", "w_ple", "w_ple_gate")
_ROW_VECTORS = ("b_in", "conv_dw_b", "conv_ln_g", "conv_ln_b", "b_ple_gate")


def kernel(x_prompt, x_sample, p_prompt, p_sample, rel_bias, ln_g, ln_b, w_ff1_gate, w_ff1_up, w_ff1_down,
           w_in, b_in, conv_dw, conv_dw_b, conv_ln_g, conv_ln_b, w_conv_out, w_attn_out, w_out, w_ff2_gate,
           w_ff2_up, w_ff2_down, w_ple, w_ple_gate, b_ple_gate):
    stacked = dict(ln_g=ln_g, ln_b=ln_b, w_ff1_gate=w_ff1_gate, w_ff1_up=w_ff1_up, w_ff1_down=w_ff1_down,
                   w_in=w_in, b_in=b_in, conv_dw=conv_dw, conv_dw_b=conv_dw_b, conv_ln_g=conv_ln_g,
                   conv_ln_b=conv_ln_b, w_conv_out=w_conv_out, w_attn_out=w_attn_out, w_out=w_out,
                   w_ff2_gate=w_ff2_gate, w_ff2_up=w_ff2_up, w_ff2_down=w_ff2_down, w_ple=w_ple,
                   w_ple_gate=w_ple_gate, b_ple_gate=b_ple_gate)
    depth = ln_g.shape[0]
    alpha = (2.0 * depth) ** 0.25
    layers = []
    for i in range(depth):
        w = {k: v[i] for k, v in stacked.items()}
        for k in _MATMUL_WEIGHTS:
            w[k] = w[k].astype(BF16)
        for k in _ROW_VECTORS:
            w[k] = w[k].reshape(1, -1)
        layers.append(w)
    biases = [_band_bias(rel_bias, gi, dil) for gi, (_, dil) in enumerate(DILATED_GROUPS)]

    def run(x, p):
        B, S, D = x.shape
        y = x.reshape(B * S, D)
        for i, w in enumerate(layers):
            y = _encoder_layer(y, p[i].reshape(B * S, -1), B, S, w, biases, alpha)
        return y.reshape(B, S, D)

    return run(x_prompt, p_prompt), run(x_sample, p_sample)
```

```python
import functools
import math

import jax
import jax.numpy as jnp
import numpy as np
from jax import lax
from jax.experimental import pallas as pl
from jax.experimental.pallas import tpu as pltpu

F32 = jnp.float32
BF16 = jnp.bfloat16

HEAD_DIM = 128
HEADS_PER_GROUP = 8
DILATED_GROUPS = ((128, 1), (512, 4), (2048, 16))
N_GROUPS = len(DILATED_GROUPS)
D_GROUP = HEADS_PER_GROUP * HEAD_DIM
D_ATTN = N_GROUPS * D_GROUP
CONV_WIDTH = 31
NUM_BUCKETS = 32
MAX_DISTANCE = 1024
LN_EPS = 1e-5
NEG_INF = -1e30

LANES = 128
SUBLANES_F32 = 8
ROWS_BF16_TILE = 16
VMEM_LIMIT_BYTES = 58 * 1024 * 1024
FFN_VMEM_LIMIT_BYTES = 63 * 1024 * 1024

ATT_R = 64
ATT_TQ = 128
CONV_HALO = 16
CONV_ROWS = 16
ACC_CHUNK_COLS = 512
EPILOGUE_ROWS = 64
NORM_SLABS = 8
FFN_NORM_SLABS = 16


def _tile(n, target, align=LANES):
    if n <= target:
        return n
    t = (target // align) * align
    while t >= align:
        if n % t == 0:
            return t
        t -= align
    raise ValueError(f"no tile for {n} <= {target} aligned to {align}")


def _params(*sem, vmem_limit_bytes=VMEM_LIMIT_BYTES):
    return pltpu.CompilerParams(dimension_semantics=sem, vmem_limit_bytes=vmem_limit_bytes)


def _layer_norm(r, g, b):
    mu = jnp.mean(r, axis=-1, keepdims=True)
    c = r - mu
    var = jnp.mean(c * c, axis=-1, keepdims=True)
    return c * lax.rsqrt(var + LN_EPS) * g + b


def _sigmoid(x):
    return 1.0 / (1.0 + jnp.exp(-x))


def _for_row_chunks(n_rows, fn):
    def body(c, carry):
        fn(pl.multiple_of(c * EPILOGUE_ROWS, EPILOGUE_ROWS))
        return carry
    lax.fori_loop(0, n_rows // EPILOGUE_ROWS, body, 0)


def _dot_into(acc_ref, terms, accumulate):
    n = acc_ref.shape[1]
    tn = _tile(n, ACC_CHUNK_COLS)
    for c in range(n // tn):
        cols = slice(c * tn, (c + 1) * tn)
        prod = None
        for a, w_ref in terms:
            term = jnp.dot(a, w_ref[:, cols], preferred_element_type=F32)
            prod = term if prod is None else prod + term
        acc_ref[:, cols] = acc_ref[:, cols] + prod if accumulate else prod


def _residual_steps(step, n_slabs, acc_ref, x_ref, scale):
    slab_rows = x_ref.shape[0]

    @pl.when(step < n_slabs)
    def _():
        slab0 = step * slab_rows

        def add(r):
            rows = pl.ds(pl.multiple_of(slab0 + r, EPILOGUE_ROWS), EPILOGUE_ROWS)
            acc_ref[rows, :] = acc_ref[rows, :] + scale * x_ref[pl.ds(r, EPILOGUE_ROWS), :]
        _for_row_chunks(slab_rows, add)


def _norm_steps(step, n_main, slab_rows, fn):
    @pl.when(step >= n_main)
    def _():
        slab0 = (step - n_main) * slab_rows
        _for_row_chunks(slab_rows, lambda r: fn(pl.ds(pl.multiple_of(slab0 + r, EPILOGUE_ROWS), EPILOGUE_ROWS),
                                                pl.ds(r, EPILOGUE_ROWS)))


def _slab_spec_factory(tm, width, n_main, n_slabs):
    rows = tm // n_slabs

    def late(cols=width):
        return pl.BlockSpec((rows, cols), lambda i, s: (i * n_slabs + jnp.clip(s - n_main, 0, n_slabs - 1), 0))

    def early(cols=width):
        return pl.BlockSpec((rows, cols), lambda i, s: (i * n_slabs + jnp.minimum(s, n_slabs - 1), 0))
    return late, early


def _ffn_kernel(xb_ref, wg0_ref, wg1_ref, wu0_ref, wu1_ref, wd0_ref, wd1_ref, x_ref, g_ref, b_ref,
                o32_ref, o16_ref, acc_ref, *, alpha, nf, n_slabs):
    s = pl.program_id(1)
    n_main = (nf + 1) // 2

    def terms(n_chunks):
        xb = xb_ref[...]
        out = []
        for wg_ref, wu_ref, wd_ref in ((wg0_ref, wu0_ref, wd0_ref), (wg1_ref, wu1_ref, wd1_ref))[:n_chunks]:
            gt = jnp.dot(xb, wg_ref[...], preferred_element_type=F32)
            up = jnp.dot(xb, wu_ref[...], preferred_element_type=F32)
            out.append((((gt * _sigmoid(gt)) * up).astype(BF16), wd_ref))
        return out

    n_full = nf // 2

    @pl.when(s == 0)
    def _():
        _dot_into(acc_ref, terms(2 if n_full > 0 else 1), accumulate=False)

    @pl.when((s > 0) & (s < n_full))
    def _():
        _dot_into(acc_ref, terms(2), accumulate=True)

    if nf % 2 and n_full > 0:
        @pl.when(s == n_full)
        def _():
            _dot_into(acc_ref, terms(1), accumulate=True)

    _residual_steps(s, n_slabs, acc_ref, x_ref, 2.0 * alpha)

    def finish(acc_rows, rows):
        y = _layer_norm(0.5 * acc_ref[acc_rows, :], g_ref[...], b_ref[...])
        o32_ref[rows, :] = y
        o16_ref[rows, :] = y.astype(BF16)

    _norm_steps(s, n_main, o32_ref.shape[0], finish)


def _ffn(x, xb, wg, wu, wd, g, b, alpha):
    M, D = x.shape
    F = wg.shape[1]
    tf = _tile(F, 256)
    nf = F // tf
    n_main = (nf + 1) // 2
    n_slabs = min(FFN_NORM_SLABS, n_main)
    tm = _tile(M, 1024, EPILOGUE_ROWS * n_slabs)
    late, early = _slab_spec_factory(tm, D, n_main, n_slabs)
    last = nf - 1
    vec = pl.BlockSpec((1, D), lambda i, s: (0, 0))

    def cols(which):
        return pl.BlockSpec((D, tf), lambda i, s: (0, jnp.minimum(2 * s + which, last)))

    def rows(which):
        return pl.BlockSpec((tf, D), lambda i, s: (jnp.minimum(2 * s + which, last), 0))

    return pl.pallas_call(
        functools.partial(_ffn_kernel, alpha=alpha, nf=nf, n_slabs=n_slabs),
        grid=(M // tm, n_main + n_slabs),
        in_specs=[
            pl.BlockSpec((tm, D), lambda i, s: (i, 0), pipeline_mode=pl.Buffered(1)),
            cols(0), cols(1), cols(0), cols(1), rows(0), rows(1),
            early(), vec, vec,
        ],
        out_specs=[late(), late()],
        out_shape=[jax.ShapeDtypeStruct((M, D), F32), jax.ShapeDtypeStruct((M, D), BF16)],
        scratch_shapes=[pltpu.VMEM((tm, D), F32)],
        compiler_params=_params("parallel", "arbitrary", vmem_limit_bytes=FFN_VMEM_LIMIT_BYTES),
        name="ffn",
    )(xb, wg, wg, wu, wu, wd, wd, x, g, b)


def _glu_proj_kernel(x_ref, wa_ref, wg_ref, ba_ref, bg_ref, o_ref):
    x = x_ref[...]
    a = jnp.dot(x, wa_ref[...], preferred_element_type=F32) + ba_ref[...]
    gt = jnp.dot(x, wg_ref[...], preferred_element_type=F32) + bg_ref[...]
    o_ref[...] = a * _sigmoid(gt)


def _glu_proj(xb, w_in, b_in, d_conv):
    M, D = xb.shape
    tm = _tile(M, 1024, ROWS_BF16_TILE)
    tn = _tile(d_conv, 512)
    off = d_conv // tn
    return pl.pallas_call(
        _glu_proj_kernel,
        grid=(M // tm, d_conv // tn),
        in_specs=[
            pl.BlockSpec((tm, D), lambda i, j: (i, 0)),
            pl.BlockSpec((D, tn), lambda i, j: (0, j)),
            pl.BlockSpec((D, tn), lambda i, j: (0, j + off)),
            pl.BlockSpec((1, tn), lambda i, j: (0, j)),
            pl.BlockSpec((1, tn), lambda i, j: (0, j + off)),
        ],
        out_specs=pl.BlockSpec((tm, tn), lambda i, j: (i, j)),
        out_shape=jax.ShapeDtypeStruct((M, d_conv), F32),
        compiler_params=_params("parallel", "arbitrary"),
        name="glu_proj",
    )(xb, w_in, w_in, b_in, b_in)


def _qkv_proj_kernel(x_ref, w_ref, b_ref, o_ref, *scratch, dilation):
    res = jnp.dot(x_ref[...], w_ref[...], preferred_element_type=F32) + b_ref[...]
    if dilation == 1:
        o_ref[0, 0] = res.astype(o_ref.dtype)
    else:
        (res_ref,) = scratch
        rows = res_ref.shape[1] // dilation
        for c in range(res_ref.shape[0]):
            lanes = slice(c * LANES, (c + 1) * LANES)
            res_ref[c] = res[:, lanes]
            for r in range(dilation):
                o_ref[0, r, :, lanes] = res_ref[c, pl.ds(r, rows, stride=dilation), :].astype(o_ref.dtype)


def _qkv_proj(xb, w_in, b_in, col_q, B, S, group, dilation):
    M, D = xb.shape
    L = S // dilation
    tm = _tile(S, 1024, ROWS_BF16_TILE * dilation)
    tn = _tile(math.gcd(D_GROUP, col_q), 512)
    per_kind = D_GROUP // tn
    tiles_per_seq = S // tm

    def w_col(i, j):
        kind = j // per_kind
        return (0, (col_q + kind * D_ATTN + group * D_GROUP) // tn + j % per_kind)

    return pl.pallas_call(
        functools.partial(_qkv_proj_kernel, dilation=dilation),
        grid=(M // tm, 3 * per_kind),
        in_specs=[
            pl.BlockSpec((tm, D), lambda i, j: (i, 0)),
            pl.BlockSpec((D, tn), w_col),
            pl.BlockSpec((1, tn), w_col),
        ],
        out_specs=pl.BlockSpec((1, dilation, tm // dilation, tn),
                               lambda i, j: (i // tiles_per_seq, 0, i % tiles_per_seq, j)),
        out_shape=jax.ShapeDtypeStruct((B, dilation, L, 3 * D_GROUP), BF16),
        scratch_shapes=[] if dilation == 1 else [pltpu.VMEM((tn // LANES, tm, LANES), F32)],
        compiler_params=_params("parallel", "arbitrary"),
        name=f"qkv_proj_d{dilation}",
    )(xb, w_in, b_in)


def _conv_kernel(prev_ref, cur_ref, next_ref, dw_ref, dwb_ref, g_ref, b_ref, o_ref, buf_ref, shift_ref,
                 *, ts, n_tiles):
    i = pl.program_id(1)
    buf_ref[0:CONV_HALO, :] = jnp.where(i > 0, prev_ref[0], 0.0)
    buf_ref[CONV_HALO:CONV_HALO + ts, :] = cur_ref[0]
    buf_ref[CONV_HALO + ts:, :] = jnp.where(i < n_tiles - 1, next_ref[0], 0.0)
    n_shift = shift_ref.shape[1]
    for s in range(1, SUBLANES_F32):
        shift_ref[s - 1] = buf_ref[s:s + n_shift, :]
    first = CONV_HALO - CONV_WIDTH // 2
    for c in range(ts // CONV_ROWS):
        acc = None
        for w in range(CONV_WIDTH):
            row = first + c * CONV_ROWS + w
            s, base = row % SUBLANES_F32, row - row % SUBLANES_F32
            src = buf_ref if s == 0 else shift_ref.at[s - 1]
            tap = jnp.concatenate([dw_ref[w]] * (CONV_ROWS // SUBLANES_F32), axis=0)
            term = src[base:base + CONV_ROWS, :] * tap
            acc = term if acc is None else acc + term
        y = _layer_norm(acc + dwb_ref[...], g_ref[...], b_ref[...])
        o_ref[0, c * CONV_ROWS:(c + 1) * CONV_ROWS, :] = (y * _sigmoid(y)).astype(o_ref.dtype)


def _conv_module(h, dw, dwb, g, b):
    B, S, C = h.shape
    ts = _tile(S, 128, CONV_HALO)
    n_tiles = S // ts
    hb = ts // CONV_HALO
    n_halo = S // CONV_HALO
    row = lambda bi, i: (0, 0)
    n_buf = ts + 2 * CONV_HALO
    taps = jnp.broadcast_to(dw[:, None, :], (CONV_WIDTH, SUBLANES_F32, C))
    return pl.pallas_call(
        functools.partial(_conv_kernel, ts=ts, n_tiles=n_tiles),
        grid=(B, n_tiles),
        in_specs=[
            pl.BlockSpec((1, CONV_HALO, C), lambda bi, i: (bi, jnp.maximum(i * hb - 1, 0), 0)),
            pl.BlockSpec((1, ts, C), lambda bi, i: (bi, i, 0)),
            pl.BlockSpec((1, CONV_HALO, C), lambda bi, i: (bi, jnp.minimum((i + 1) * hb, n_halo - 1), 0)),
            pl.BlockSpec((CONV_WIDTH, SUBLANES_F32, C), lambda bi, i: (0, 0, 0)),
            pl.BlockSpec((1, C), row),
            pl.BlockSpec((1, C), row),
            pl.BlockSpec((1, C), row),
        ],
        out_specs=pl.BlockSpec((1, ts, C), lambda bi, i: (bi, i, 0)),
        out_shape=jax.ShapeDtypeStruct((B, S, C), BF16),
        scratch_shapes=[pltpu.VMEM((n_buf, C), F32),
                        pltpu.VMEM((SUBLANES_F32 - 1, n_buf - SUBLANES_F32, C), F32)],
        compiler_params=_params("parallel", "arbitrary"),
        name="conv_module",
    )(h, h, h, taps, dwb, g, b)


def _t5_bucket(rel):
    half = NUM_BUCKETS // 2
    max_exact = half // 2
    ret = (rel > 0).astype(np.int32) * half
    n = np.abs(rel)
    large = max_exact + (np.log(np.maximum(n, 1) / max_exact) / np.log(MAX_DISTANCE / max_exact)
                         * (half - max_exact)).astype(np.int32)
    large = np.minimum(large, half - 1)
    return (ret + np.where(n < max_exact, n, large)).astype(np.int32)


def _band_bias(rel_bias, group, dilation):
    qi = np.arange(ATT_TQ)[:, None]
    kj = np.arange(ATT_TQ + 2 * ATT_R)[None, :]
    rel = kj - ATT_R - qi
    band = np.abs(rel) <= ATT_R
    bucket = _t5_bucket(np.clip(rel, -ATT_R, ATT_R) * dilation)
    table = rel_bias[:, group * HEADS_PER_GROUP:(group + 1) * HEADS_PER_GROUP].astype(F32)
    onehot = bucket[None, :, :, None] == np.arange(NUM_BUCKETS)
    bias = jnp.sum(jnp.where(onehot, table.T[:, None, None, :], 0.0), axis=-1)
    return jnp.where(band[None], bias, NEG_INF)


def _attn_kernel(q_ref, kp_ref, kc_ref, kn_ref, vp_ref, vc_ref, vn_ref, bias_ref, o_ref, lse_ref,
                 kbuf_ref, vbuf_ref, *, sub_len, scale):
    i = pl.program_id(2)
    tq = ATT_TQ
    for buf, p, c, n in ((kbuf_ref, kp_ref, kc_ref, kn_ref), (vbuf_ref, vp_ref, vc_ref, vn_ref)):
        buf[0:ATT_R, :] = p[0, 0]
        buf[ATT_R:ATT_R + tq, :] = c[0, 0]
        buf[ATT_R + tq:, :] = n[0, 0]
    key_pos = i * tq - ATT_R + lax.broadcasted_iota(jnp.int32, (1, tq + 2 * ATT_R), 1)
    valid = (key_pos >= 0) & (key_pos < sub_len)
    for h in range(HEADS_PER_GROUP):
        cols = slice(h * HEAD_DIM, (h + 1) * HEAD_DIM)
        s = lax.dot_general(q_ref[0, 0, :, cols], kbuf_ref[:, cols], (((1,), (1,)), ((), ())),
                            preferred_element_type=F32) * scale
        s = jnp.where(valid, s + bias_ref[h], NEG_INF)
        m = jnp.max(s, axis=-1, keepdims=True)
        p = jnp.exp(s - m)
        l = jnp.sum(p, axis=-1, keepdims=True)
        p = p * (1.0 / l)
        o_ref[0, 0, :, cols] = jnp.dot(p.astype(BF16), vbuf_ref[:, cols], preferred_element_type=F32)
        lse_ref[0, 0, :, cols] = jnp.broadcast_to(m + jnp.log(l), (tq, HEAD_DIM))


def _attn_group(qkv, bias):
    B, dilation, L, _ = qkv.shape
    tq = ATT_TQ
    assert L % tq == 0, (L, tq)
    hb = tq // ATT_R
    n_halo = L // ATT_R

    def cur(kind):
        return pl.BlockSpec((1, 1, tq, D_GROUP), lambda b, r, i: (b, r, i, kind))

    def before(kind):
        return pl.BlockSpec((1, 1, ATT_R, D_GROUP), lambda b, r, i: (b, r, jnp.maximum(i * hb - 1, 0), kind))

    def after(kind):
        return pl.BlockSpec((1, 1, ATT_R, D_GROUP),
                            lambda b, r, i: (b, r, jnp.minimum((i + 1) * hb, n_halo - 1), kind))

    out_spec = pl.BlockSpec((1, 1, tq, D_GROUP), lambda b, r, i: (b, r, i, 0))
    out_sds = jax.ShapeDtypeStruct((B, dilation, L, D_GROUP), F32)
    return pl.pallas_call(
        functools.partial(_attn_kernel, sub_len=L, scale=HEAD_DIM ** -0.5),
        grid=(B, dilation, L // tq),
        in_specs=[cur(0), before(1), cur(1), after(1), before(2), cur(2), after(2),
                  pl.BlockSpec((HEADS_PER_GROUP, tq, tq + 2 * ATT_R), lambda b, r, i: (0, 0, 0))],
        out_specs=[out_spec, out_spec],
        out_shape=[out_sds, out_sds],
        scratch_shapes=[pltpu.VMEM((tq + 2 * ATT_R, D_GROUP), BF16), pltpu.VMEM((tq + 2 * ATT_R, D_GROUP), BF16)],
        compiler_params=_params("parallel", "parallel", "arbitrary"),
        name=f"attn_d{dilation}",
    )(qkv, qkv, qkv, qkv, qkv, qkv, qkv, bias)


def _combine_kernel(*refs):
    n = N_GROUPS
    o_refs, l_refs, out_ref, buf_ref = refs[:n], refs[n:2 * n], refs[2 * n], refs[2 * n + 1]

    def token_order(ref, slot, lanes):
        dilation, rows = ref.shape[1], ref.shape[2]
        if dilation == 1:
            return ref[0, 0, :, lanes]
        for r in range(dilation):
            buf_ref[slot, pl.ds(r, rows, stride=dilation), :] = ref[0, r, :, lanes]
        return buf_ref[slot]

    for c in range(out_ref.shape[2] // LANES):
        lanes = slice(c * LANES, (c + 1) * LANES)
        outs = [token_order(ref, g, lanes) for g, ref in enumerate(o_refs)]
        lses = [token_order(ref, n + g, lanes) for g, ref in enumerate(l_refs)]
        mx = functools.reduce(jnp.maximum, lses)
        es = [jnp.exp(l - mx) for l in lses]
        den = functools.reduce(lambda a, b: a + b, es)
        acc = None
        for e, o in zip(es, outs):
            term = (e / den) * o
            acc = term if acc is None else acc + term
        out_ref[0, :, lanes] = acc.astype(out_ref.dtype)


def _combine(outs, lses, S):
    B = outs[0].shape[0]
    N = outs[0].shape[-1]
    max_dil = max(o.shape[1] for o in outs)
    tm = _tile(S, 256, SUBLANES_F32 * max_dil)

    def spec(a):
        d = a.shape[1]
        return pl.BlockSpec((1, d, tm // d, N), lambda b, i: (b, 0, i, 0))

    return pl.pallas_call(
        _combine_kernel,
        grid=(B, S // tm),
        in_specs=[spec(a) for a in (*outs, *lses)],
        out_specs=pl.BlockSpec((1, tm, N), lambda b, i: (b, i, 0)),
        out_shape=jax.ShapeDtypeStruct((B, S, N), BF16),
        scratch_shapes=[pltpu.VMEM((2 * N_GROUPS, tm, LANES), F32)],
        compiler_params=_params("parallel", "arbitrary"),
        name="attn_combine",
    )(*outs, *lses)


def _merge_kernel(x_ref, hc_ref, at_ref, wg0_ref, wg1_ref, wc_ref, wa_ref, b0_ref, b1_ref, o_ref):
    x = x_ref[...]
    g0 = _sigmoid(jnp.dot(x, wg0_ref[...], preferred_element_type=F32) + b0_ref[...])
    g1 = _sigmoid(jnp.dot(x, wg1_ref[...], preferred_element_type=F32) + b1_ref[...])
    conv_out = jnp.dot(hc_ref[...], wc_ref[...], preferred_element_type=F32)
    attn_out = jnp.dot(at_ref[...], wa_ref[...], preferred_element_type=F32)
    o_ref[...] = (g0 * conv_out + g1 * attn_out).astype(o_ref.dtype)


def _merge(xb, hc, attn, w_in, b_in, w_conv_out, w_attn_out, col_gate):
    M, D = xb.shape
    tm = _tile(M, 1024, ROWS_BF16_TILE)
    tn = _tile(math.gcd(D, col_gate), 256)
    off0 = col_gate // tn
    off1 = (col_gate + D) // tn
    return pl.pallas_call(
        _merge_kernel,
        grid=(M // tm, D // tn),
        in_specs=[
            pl.BlockSpec((tm, D), lambda i, j: (i, 0)),
            pl.BlockSpec((tm, hc.shape[1]), lambda i, j: (i, 0)),
            pl.BlockSpec((tm, attn.shape[1]), lambda i, j: (i, 0)),
            pl.BlockSpec((D, tn), lambda i, j: (0, j + off0)),
            pl.BlockSpec((D, tn), lambda i, j: (0, j + off1)),
            pl.BlockSpec((hc.shape[1], tn), lambda i, j: (0, j)),
            pl.BlockSpec((attn.shape[1], tn), lambda i, j: (0, j)),
            pl.BlockSpec((1, tn), lambda i, j: (0, j + off0)),
            pl.BlockSpec((1, tn), lambda i, j: (0, j + off1)),
        ],
        out_specs=pl.BlockSpec((tm, tn), lambda i, j: (i, j)),
        out_shape=jax.ShapeDtypeStruct((M, D), BF16),
        compiler_params=_params("parallel", "arbitrary"),
        name="merge",
    )(xb, hc, attn, w_in, w_in, w_conv_out, w_attn_out, b_in, b_in)


def _zero_rows(ref, n_rows):
    def zero(r):
        ref[pl.ds(r, EPILOGUE_ROWS), :] = jnp.zeros((EPILOGUE_ROWS, ref.shape[1]), ref.dtype)
    _for_row_chunks(n_rows, zero)


def _deferred_norm_steps(n_tiles, acc_refs, main, norm):
    i = pl.program_id(0)
    k = pl.program_id(1)

    @pl.when((i == 0) & (k == 0))
    def _():
        for acc_ref in acc_refs:
            _zero_rows(acc_ref, acc_ref.shape[0])

    @pl.when(i == 0)
    def _():
        main(acc_refs[0])

    for parity in (0, 1):
        @pl.when((i > 0) & (i < n_tiles) & (i % 2 == parity))
        def _():
            norm(acc_refs[1 - parity])
            main(acc_refs[parity])

    @pl.when(i == n_tiles)
    def _():
        norm(acc_refs[(n_tiles - 1) % 2])


def _slab_chunks(k, slab_rows):
    for c in range(slab_rows // EPILOGUE_ROWS):
        r = c * EPILOGUE_ROWS
        yield (pl.ds(pl.multiple_of(k * slab_rows + r, EPILOGUE_ROWS), EPILOGUE_ROWS), pl.ds(r, EPILOGUE_ROWS))


def _deferred_specs(n_tiles, n_steps, tm, D):
    rows = tm // n_steps

    def main_step(i, k):
        return jnp.where(i < n_tiles, k, n_steps - 1)

    def main_tile(i):
        return jnp.minimum(i, n_tiles - 1)

    def a_spec(tk):
        return pl.BlockSpec((tm, tk), lambda i, k: (main_tile(i), main_step(i, k)))

    def w_spec(tk):
        return pl.BlockSpec((tk, D), lambda i, k: (main_step(i, k), 0))

    def main_slab(cols=D):
        return pl.BlockSpec((rows, cols), lambda i, k: (main_tile(i) * n_steps + main_step(i, k), 0))

    def norm_slab(cols=D):
        return pl.BlockSpec((rows, cols), lambda i, k: (jnp.maximum(i - 1, 0) * n_steps + jnp.where(i > 0, k, 0), 0))

    return a_spec, w_spec, main_slab, norm_slab


def _out_proj_kernel(a_ref, w_ref, x_ref, g_ref, b_ref, o32_ref, o16_ref, acc0_ref, acc1_ref, *, alpha, n_tiles):
    k = pl.program_id(1)
    slab_rows = x_ref.shape[0]

    def main(acc_ref):
        _dot_into(acc_ref, [(a_ref[...], w_ref)], accumulate=True)
        for acc_rows, rows in _slab_chunks(k, slab_rows):
            acc_ref[acc_rows, :] = acc_ref[acc_rows, :] + alpha * x_ref[rows, :]

    def norm(acc_ref):
        for acc_rows, rows in _slab_chunks(k, slab_rows):
            y = _layer_norm(acc_ref[acc_rows, :], g_ref[...], b_ref[...])
            o32_ref[rows, :] = y
            o16_ref[rows, :] = y.astype(BF16)
            acc_ref[acc_rows, :] = jnp.zeros_like(y)

    _deferred_norm_steps(n_tiles, (acc0_ref, acc1_ref), main, norm)


def _out_proj(a, w, x, g, b, alpha):
    M, K = a.shape
    D = w.shape[1]
    tk = _tile(K, 1024)
    nk = K // tk
    tm = _tile(M, 512, EPILOGUE_ROWS * nk)
    n_tiles = M // tm
    a_spec, w_spec, main_slab, norm_slab = _deferred_specs(n_tiles, nk, tm, D)
    vec = pl.BlockSpec((1, D), lambda i, k: (0, 0))
    return pl.pallas_call(
        functools.partial(_out_proj_kernel, alpha=alpha, n_tiles=n_tiles),
        grid=(n_tiles + 1, nk),
        in_specs=[a_spec(tk), w_spec(tk), main_slab(), vec, vec],
        out_specs=[norm_slab(), norm_slab()],
        out_shape=[jax.ShapeDtypeStruct((M, D), F32), jax.ShapeDtypeStruct((M, D), BF16)],
        scratch_shapes=[pltpu.VMEM((tm, D), F32), pltpu.VMEM((tm, D), F32)],
        compiler_params=_params("arbitrary", "arbitrary"),
        name="out_proj",
    )(a, w, x, g, b)


def _ple_kernel(a_ref, w_ref, x_ref, p_ref, wp_ref, bg_ref, g_ref, b_ref, o_ref, acc0_ref, acc1_ref,
                *, alpha, n_tiles):
    k = pl.program_id(1)
    slab_rows = x_ref.shape[0]

    def main(acc_ref):
        _dot_into(acc_ref, [(a_ref[...], w_ref)], accumulate=True)

    def norm(acc_ref):
        for acc_rows, rows in _slab_chunks(k, slab_rows):
            gate = _sigmoid(acc_ref[acc_rows, :] + bg_ref[...])
            pw = jnp.dot(p_ref[rows, :].astype(BF16), wp_ref[...], preferred_element_type=F32)
            o_ref[rows, :] = _layer_norm(alpha * x_ref[rows, :] + pw * gate, g_ref[...], b_ref[...])
            acc_ref[acc_rows, :] = jnp.zeros_like(gate)

    _deferred_norm_steps(n_tiles, (acc0_ref, acc1_ref), main, norm)


def _ple(a, w, x, p, wp, bg, g, b, alpha):
    M, K = a.shape
    D = w.shape[1]
    P = p.shape[1]
    tk = _tile(K, 1024)
    nk = K // tk
    tm = _tile(M, 512, EPILOGUE_ROWS * nk)
    n_tiles = M // tm
    a_spec, w_spec, _, norm_slab = _deferred_specs(n_tiles, nk, tm, D)
    vec = pl.BlockSpec((1, D), lambda i, k: (0, 0))
    return pl.pallas_call(
        functools.partial(_ple_kernel, alpha=alpha, n_tiles=n_tiles),
        grid=(n_tiles + 1, nk),
        in_specs=[a_spec(tk), w_spec(tk), norm_slab(), norm_slab(P),
                  pl.BlockSpec((P, D), lambda i, k: (0, 0)), vec, vec, vec],
        out_specs=norm_slab(),
        out_shape=jax.ShapeDtypeStruct((M, D), F32),
        scratch_shapes=[pltpu.VMEM((tm, D), F32), pltpu.VMEM((tm, D), F32)],
        compiler_params=_params("arbitrary", "arbitrary"),
        name="ple",
    )(a, w, x, p, wp, bg, g, b)


def _encoder_layer(x, p, B, S, w, biases, alpha):
    d_conv = w["conv_dw"].shape[1]
    col_q = 2 * d_conv
    col_gate = col_q + 3 * D_ATTN
    ln_g, ln_b = w["ln_g"], w["ln_b"]

    x1, x1b = _ffn(x, x.astype(BF16), w["w_ff1_gate"], w["w_ff1_up"], w["w_ff1_down"], ln_g[0:1], ln_b[0:1], alpha)

    h = _glu_proj(x1b, w["w_in"], w["b_in"], d_conv)
    hc = _conv_module(h.reshape(B, S, d_conv), w["conv_dw"], w["conv_dw_b"], w["conv_ln_g"], w["conv_ln_b"])
    hc = hc.reshape(B * S, d_conv)

    outs, lses = [], []
    for gi, (_, dil) in enumerate(DILATED_GROUPS):
        qkv = _qkv_proj(x1b, w["w_in"], w["b_in"], col_q, B, S, gi, dil)
        o, lse = _attn_group(qkv, biases[gi])
        outs.append(o)
        lses.append(lse)
    attn = _combine(outs, lses, S).reshape(B * S, D_GROUP)

    merged = _merge(x1b, hc, attn, w["w_in"], w["b_in"], w["w_conv_out"], w["w_attn_out"], col_gate)
    x2, x2b = _out_proj(merged, w["w_out"], x1, ln_g[1:2], ln_b[1:2], alpha)
    x3, x3b = _ffn(x2, x2b, w["w_ff2_gate"], w["w_ff2_up"], w["w_ff2_down"], ln_g[2:3], ln_b[2:3], alpha)
    return _ple(x3b, w["w_ple_gate"], x3, p, w["w_ple"], w["b_ple_gate"], ln_g[3:4], ln_b[3:4], alpha)


_MATMUL_WEIGHTS = ("w_ff1_gate", "w_ff1_up", "w_ff1_down", "w_in", "w_conv_out", "w_attn_out", "w_out",
                   "w_ff2_gate", "w_ff2_up", "w_ff2_down", "w_ple", "w_ple_gate")
_ROW_VECTORS = ("b_in", "conv_dw_b", "conv_ln_g", "conv_ln_b", "b_ple_gate")


def kernel(x_prompt, x_sample, p_prompt, p_sample, rel_bias, ln_g, ln_b, w_ff1_gate, w_ff1_up, w_ff1_down,
           w_in, b_in, conv_dw, conv_dw_b, conv_ln_g, conv_ln_b, w_conv_out, w_attn_out, w_out, w_ff2_gate,
           w_ff2_up, w_ff2_down, w_ple, w_ple_gate, b_ple_gate):
    stacked = dict(ln_g=ln_g, ln_b=ln_b, w_ff1_gate=w_ff1_gate, w_ff1_up=w_ff1_up, w_ff1_down=w_ff1_down,
                   w_in=w_in, b_in=b_in, conv_dw=conv_dw, conv_dw_b=conv_dw_b, conv_ln_g=conv_ln_g,
                   conv_ln_b=conv_ln_b, w_conv_out=w_conv_out, w_attn_out=w_attn_out, w_out=w_out,
                   w_ff2_gate=w_ff2_gate, w_ff2_up=w_ff2_up, w_ff2_down=w_ff2_down, w_ple=w_ple,
                   w_ple_gate=w_ple_gate, b_ple_gate=b_ple_gate)
    depth = ln_g.shape[0]
    alpha = (2.0 * depth) ** 0.25
    layers = []
    for i in range(depth):
        w = {k: v[i] for k, v in stacked.items()}
        for k in _MATMUL_WEIGHTS:
            w[k] = w[k].astype(BF16)
        for k in _ROW_VECTORS:
            w[k] = w[k].reshape(1, -1)
        layers.append(w)
    biases = [_band_bias(rel_bias, gi, dil) for gi, (_, dil) in enumerate(DILATED_GROUPS)]

    def run(x, p):
        B, S, D = x.shape
        y = x.reshape(B * S, D)
        for i, w in enumerate(layers):
            y = _encoder_layer(y, p[i].reshape(B * S, -1), B, S, w, biases, alpha)
        return y.reshape(B, S, D)

    return run(x_prompt, p_prompt), run(x_sample, p_sample)
```

```python
import functools
import math

import jax
import jax.numpy as jnp
import numpy as np
from jax import lax
from jax.experimental import pallas as pl
from jax.experimental.pallas import tpu as pltpu

F32 = jnp.float32
BF16 = jnp.bfloat16

HEAD_DIM = 128
HEADS_PER_GROUP = 8
DILATED_GROUPS = ((128, 1), (512, 4), (2048, 16))
N_GROUPS = len(DILATED_GROUPS)
D_GROUP = HEADS_PER_GROUP * HEAD_DIM
D_ATTN = N_GROUPS * D_GROUP
CONV_WIDTH = 31
NUM_BUCKETS = 32
MAX_DISTANCE = 1024
LN_EPS = 1e-5
NEG_INF = -1e30

LANES = 128
SUBLANES_F32 = 8
ROWS_BF16_TILE = 16
MAX_CLEAN_STRIDE = 4
VMEM_LIMIT_BYTES = 58 * 1024 * 1024
FFN_VMEM_LIMIT_BYTES = 63 * 1024 * 1024

ATT_R = 64
ATT_TQ = 128
CONV_HALO = 16
CONV_ROWS = 16
ACC_CHUNK_COLS = 512
EPILOGUE_ROWS = 64
LN_CHUNK_COLS = 512
FFN_NORM_SLABS = 16


def _tile(n, target, align=LANES):
    if n <= target:
        return n
    t = (target // align) * align
    while t >= align:
        if n % t == 0:
            return t
        t -= align
    raise ValueError(f"no tile for {n} <= {target} aligned to {align}")


def _params(*sem, vmem_limit_bytes=VMEM_LIMIT_BYTES):
    return pltpu.CompilerParams(dimension_semantics=sem, vmem_limit_bytes=vmem_limit_bytes)


def _layer_norm(r, g, b):
    mu = jnp.mean(r, axis=-1, keepdims=True)
    c = r - mu
    var = jnp.mean(c * c, axis=-1, keepdims=True)
    return c * lax.rsqrt(var + LN_EPS) * g + b


def _layer_norm_chunked(read, n_cols, g_ref, b_ref, write):
    chunks = [slice(c, c + LN_CHUNK_COLS) for c in range(0, n_cols, LN_CHUNK_COLS)]
    total = functools.reduce(lambda a, b: a + b, [read(cols) for cols in chunks])
    mu = jnp.sum(total, axis=-1, keepdims=True) / n_cols
    sq = None
    for cols in chunks:
        d = read(cols) - mu
        sq = d * d if sq is None else sq + d * d
    rstd = lax.rsqrt(jnp.sum(sq, axis=-1, keepdims=True) / n_cols + LN_EPS)
    for cols in chunks:
        write(cols, (read(cols) - mu) * rstd * g_ref[:, cols] + b_ref[:, cols])


def _sigmoid(x):
    return 1.0 / (1.0 + jnp.exp(-x))


def _for_row_chunks(n_rows, fn):
    def body(c, carry):
        fn(pl.multiple_of(c * EPILOGUE_ROWS, EPILOGUE_ROWS))
        return carry
    lax.fori_loop(0, n_rows // EPILOGUE_ROWS, body, 0)


def _dot_into(acc_ref, terms, accumulate):
    n = acc_ref.shape[1]
    tn = _tile(n, ACC_CHUNK_COLS)
    for c in range(n // tn):
        cols = slice(c * tn, (c + 1) * tn)
        prod = None
        for a, w_ref in terms:
            term = jnp.dot(a, w_ref[:, cols], preferred_element_type=F32)
            prod = term if prod is None else prod + term
        acc_ref[:, cols] = acc_ref[:, cols] + prod if accumulate else prod


def _residual_steps(step, n_slabs, acc_ref, x_ref, scale):
    slab_rows = x_ref.shape[0]

    @pl.when(step < n_slabs)
    def _():
        slab0 = step * slab_rows

        def add(r):
            rows = pl.ds(pl.multiple_of(slab0 + r, EPILOGUE_ROWS), EPILOGUE_ROWS)
            acc_ref[rows, :] = acc_ref[rows, :] + scale * x_ref[pl.ds(r, EPILOGUE_ROWS), :]
        _for_row_chunks(slab_rows, add)


def _norm_steps(step, n_main, slab_rows, fn):
    @pl.when(step >= n_main)
    def _():
        slab0 = (step - n_main) * slab_rows
        _for_row_chunks(slab_rows, lambda r: fn(pl.ds(pl.multiple_of(slab0 + r, EPILOGUE_ROWS), EPILOGUE_ROWS),
                                                pl.ds(r, EPILOGUE_ROWS)))


def _slab_spec_factory(tm, width, n_main, n_slabs):
    rows = tm // n_slabs

    def late(cols=width):
        return pl.BlockSpec((rows, cols), lambda i, s: (i * n_slabs + jnp.clip(s - n_main, 0, n_slabs - 1), 0))

    def early(cols=width):
        return pl.BlockSpec((rows, cols), lambda i, s: (i * n_slabs + jnp.minimum(s, n_slabs - 1), 0))
    return late, early


def _ffn_kernel(xb_ref, wg0_ref, wg1_ref, wu0_ref, wu1_ref, wd0_ref, wd1_ref, x_ref, g_ref, b_ref,
                o32_ref, o16_ref, acc_ref, *, alpha, nf, n_slabs):
    s = pl.program_id(1)
    n_main = (nf + 1) // 2

    def terms(n_chunks):
        xb = xb_ref[...]
        out = []
        for wg_ref, wu_ref, wd_ref in ((wg0_ref, wu0_ref, wd0_ref), (wg1_ref, wu1_ref, wd1_ref))[:n_chunks]:
            gt = jnp.dot(xb, wg_ref[...], preferred_element_type=F32)
            up = jnp.dot(xb, wu_ref[...], preferred_element_type=F32)
            out.append((((gt * _sigmoid(gt)) * up).astype(BF16), wd_ref))
        return out

    n_full = nf // 2

    @pl.when(s == 0)
    def _():
        _dot_into(acc_ref, terms(2 if n_full > 0 else 1), accumulate=False)

    @pl.when((s > 0) & (s < n_full))
    def _():
        _dot_into(acc_ref, terms(2), accumulate=True)

    if nf % 2 and n_full > 0:
        @pl.when(s == n_full)
        def _():
            _dot_into(acc_ref, terms(1), accumulate=True)

    _residual_steps(s, n_slabs, acc_ref, x_ref, 2.0 * alpha)

    def finish(acc_rows, rows):
        def write(cols, y):
            o32_ref[rows, cols] = y
            o16_ref[rows, cols] = y.astype(BF16)
        _layer_norm_chunked(lambda cols: 0.5 * acc_ref[acc_rows, cols], acc_ref.shape[1], g_ref, b_ref, write)

    _norm_steps(s, n_main, o32_ref.shape[0], finish)


def _ffn(x, xb, wg, wu, wd, g, b, alpha):
    M, D = x.shape
    F = wg.shape[1]
    tf = _tile(F, 256)
    nf = F // tf
    n_main = (nf + 1) // 2
    n_slabs = min(FFN_NORM_SLABS, n_main)
    tm = _tile(M, 1024, EPILOGUE_ROWS * n_slabs)
    late, early = _slab_spec_factory(tm, D, n_main, n_slabs)
    last = nf - 1
    vec = pl.BlockSpec((1, D), lambda i, s: (0, 0))

    def cols(which):
        return pl.BlockSpec((D, tf), lambda i, s: (0, jnp.minimum(2 * s + which, last)))

    def rows(which):
        return pl.BlockSpec((tf, D), lambda i, s: (jnp.minimum(2 * s + which, last), 0))

    return pl.pallas_call(
        functools.partial(_ffn_kernel, alpha=alpha, nf=nf, n_slabs=n_slabs),
        grid=(M // tm, n_main + n_slabs),
        in_specs=[
            pl.BlockSpec((tm, D), lambda i, s: (i, 0), pipeline_mode=pl.Buffered(1)),
            cols(0), cols(1), cols(0), cols(1), rows(0), rows(1),
            early(), vec, vec,
        ],
        out_specs=[late(), late()],
        out_shape=[jax.ShapeDtypeStruct((M, D), F32), jax.ShapeDtypeStruct((M, D), BF16)],
        scratch_shapes=[pltpu.VMEM((tm, D), F32)],
        compiler_params=_params("parallel", "arbitrary", vmem_limit_bytes=FFN_VMEM_LIMIT_BYTES),
        name="ffn",
    )(xb, wg, wg, wu, wu, wd, wd, x, g, b)


def _glu_proj_kernel(x_ref, wa_ref, wg_ref, ba_ref, bg_ref, o_ref):
    x = x_ref[...]
    a = jnp.dot(x, wa_ref[...], preferred_element_type=F32) + ba_ref[...]
    gt = jnp.dot(x, wg_ref[...], preferred_element_type=F32) + bg_ref[...]
    o_ref[...] = a * _sigmoid(gt)


def _glu_proj(xb, w_in, b_in, d_conv):
    M, D = xb.shape
    tm = _tile(M, 1024, ROWS_BF16_TILE)
    tn = _tile(d_conv, 512)
    off = d_conv // tn
    return pl.pallas_call(
        _glu_proj_kernel,
        grid=(M // tm, d_conv // tn),
        in_specs=[
            pl.BlockSpec((tm, D), lambda i, j: (i, 0)),
            pl.BlockSpec((D, tn), lambda i, j: (0, j)),
            pl.BlockSpec((D, tn), lambda i, j: (0, j + off)),
            pl.BlockSpec((1, tn), lambda i, j: (0, j)),
            pl.BlockSpec((1, tn), lambda i, j: (0, j + off)),
        ],
        out_specs=pl.BlockSpec((tm, tn), lambda i, j: (i, j)),
        out_shape=jax.ShapeDtypeStruct((M, d_conv), F32),
        compiler_params=_params("parallel", "arbitrary"),
        name="glu_proj",
    )(xb, w_in, w_in, b_in, b_in)


def _qkv_proj_kernel(x_ref, w_ref, b_ref, o_ref, *scratch, dilation):
    res = jnp.dot(x_ref[...], w_ref[...], preferred_element_type=F32) + b_ref[...]
    if dilation == 1:
        o_ref[0, 0] = res.astype(o_ref.dtype)
    else:
        res_ref, tmp_ref = scratch
        tile_rows = res_ref.shape[1]
        d1 = tmp_ref.shape[0]
        d2 = dilation // d1
        for c in range(res_ref.shape[0]):
            lanes = slice(c * LANES, (c + 1) * LANES)
            res_ref[c] = res[:, lanes]
            for r1 in range(d1):
                if d2 == 1:
                    o_ref[0, r1, :, lanes] = res_ref[c, pl.ds(r1, tile_rows // d1, stride=d1), :].astype(o_ref.dtype)
                else:
                    tmp_ref[r1] = res_ref[c, pl.ds(r1, tile_rows // d1, stride=d1), :]
            if d2 > 1:
                for r1 in range(d1):
                    for r2 in range(d2):
                        o_ref[0, r2 * d1 + r1, :, lanes] = (
                            tmp_ref[r1, pl.ds(r2, tile_rows // dilation, stride=d2), :].astype(o_ref.dtype))


def _qkv_proj(xb, w_in, b_in, col_q, B, S, group, dilation):
    M, D = xb.shape
    L = S // dilation
    tm = _tile(S, 1024, ROWS_BF16_TILE * dilation)
    tn = _tile(math.gcd(D_GROUP, col_q), 512)
    per_kind = D_GROUP // tn
    tiles_per_seq = S // tm
    d1 = min(dilation, MAX_CLEAN_STRIDE)
    assert dilation % d1 == 0 and dilation // d1 <= MAX_CLEAN_STRIDE, dilation

    def w_col(i, j):
        kind = j // per_kind
        return (0, (col_q + kind * D_ATTN + group * D_GROUP) // tn + j % per_kind)

    return pl.pallas_call(
        functools.partial(_qkv_proj_kernel, dilation=dilation),
        grid=(M // tm, 3 * per_kind),
        in_specs=[
            pl.BlockSpec((tm, D), lambda i, j: (i, 0)),
            pl.BlockSpec((D, tn), w_col),
            pl.BlockSpec((1, tn), w_col),
        ],
        out_specs=pl.BlockSpec((1, dilation, tm // dilation, tn),
                               lambda i, j: (i // tiles_per_seq, 0, i % tiles_per_seq, j)),
        out_shape=jax.ShapeDtypeStruct((B, dilation, L, 3 * D_GROUP), BF16),
        scratch_shapes=[] if dilation == 1 else [pltpu.VMEM((tn // LANES, tm, LANES), F32),
                                                 pltpu.VMEM((d1, tm // d1, LANES), F32)],
        compiler_params=_params("parallel", "arbitrary"),
        name=f"qkv_proj_d{dilation}",
    )(xb, w_in, b_in)


def _conv_kernel(prev_ref, cur_ref, next_ref, dw_ref, dwb_ref, g_ref, b_ref, o_ref, buf_ref, shift_ref,
                 *, ts, n_tiles):
    i = pl.program_id(1)
    buf_ref[0:CONV_HALO, :] = jnp.where(i > 0, prev_ref[0], 0.0)
    buf_ref[CONV_HALO:CONV_HALO + ts, :] = cur_ref[0]
    buf_ref[CONV_HALO + ts:, :] = jnp.where(i < n_tiles - 1, next_ref[0], 0.0)
    n_shift = shift_ref.shape[1]
    for s in range(1, SUBLANES_F32):
        shift_ref[s - 1] = buf_ref[s:s + n_shift, :]
    first = CONV_HALO - CONV_WIDTH // 2
    for c in range(ts // CONV_ROWS):
        acc = None
        for w in range(CONV_WIDTH):
            row = first + c * CONV_ROWS + w
            s, base = row % SUBLANES_F32, row - row % SUBLANES_F32
            src = buf_ref if s == 0 else shift_ref.at[s - 1]
            tap = jnp.concatenate([dw_ref[w]] * (CONV_ROWS // SUBLANES_F32), axis=0)
            term = src[base:base + CONV_ROWS, :] * tap
            acc = term if acc is None else acc + term
        y = _layer_norm(acc + dwb_ref[...], g_ref[...], b_ref[...])
        o_ref[0, c * CONV_ROWS:(c + 1) * CONV_ROWS, :] = (y * _sigmoid(y)).astype(o_ref.dtype)


def _conv_module(h, dw, dwb, g, b):
    B, S, C = h.shape
    ts = _tile(S, 128, CONV_HALO)
    n_tiles = S // ts
    hb = ts // CONV_HALO
    n_halo = S // CONV_HALO
    row = lambda bi, i: (0, 0)
    n_buf = ts + 2 * CONV_HALO
    taps = jnp.broadcast_to(dw[:, None, :], (CONV_WIDTH, SUBLANES_F32, C))
    return pl.pallas_call(
        functools.partial(_conv_kernel, ts=ts, n_tiles=n_tiles),
        grid=(B, n_tiles),
        in_specs=[
            pl.BlockSpec((1, CONV_HALO, C), lambda bi, i: (bi, jnp.maximum(i * hb - 1, 0), 0)),
            pl.BlockSpec((1, ts, C), lambda bi, i: (bi, i, 0)),
            pl.BlockSpec((1, CONV_HALO, C), lambda bi, i: (bi, jnp.minimum((i + 1) * hb, n_halo - 1), 0)),
            pl.BlockSpec((CONV_WIDTH, SUBLANES_F32, C), lambda bi, i: (0, 0, 0)),
            pl.BlockSpec((1, C), row),
            pl.BlockSpec((1, C), row),
            pl.BlockSpec((1, C), row),
        ],
        out_specs=pl.BlockSpec((1, ts, C), lambda bi, i: (bi, i, 0)),
        out_shape=jax.ShapeDtypeStruct((B, S, C), BF16),
        scratch_shapes=[pltpu.VMEM((n_buf, C), F32),
                        pltpu.VMEM((SUBLANES_F32 - 1, n_buf - SUBLANES_F32, C), F32)],
        compiler_params=_params("parallel", "arbitrary"),
        name="conv_module",
    )(h, h, h, taps, dwb, g, b)


def _t5_bucket(rel):
    half = NUM_BUCKETS // 2
    max_exact = half // 2
    ret = (rel > 0).astype(np.int32) * half
    n = np.abs(rel)
    large = max_exact + (np.log(np.maximum(n, 1) / max_exact) / np.log(MAX_DISTANCE / max_exact)
                         * (half - max_exact)).astype(np.int32)
    large = np.minimum(large, half - 1)
    return (ret + np.where(n < max_exact, n, large)).astype(np.int32)


def _band_bias(rel_bias, group, dilation):
    qi = np.arange(ATT_TQ)[:, None]
    kj = np.arange(ATT_TQ + 2 * ATT_R)[None, :]
    rel = kj - ATT_R - qi
    band = np.abs(rel) <= ATT_R
    bucket = _t5_bucket(np.clip(rel, -ATT_R, ATT_R) * dilation)
    table = rel_bias[:, group * HEADS_PER_GROUP:(group + 1) * HEADS_PER_GROUP].astype(F32)
    onehot = bucket[None, :, :, None] == np.arange(NUM_BUCKETS)
    bias = jnp.sum(jnp.where(onehot, table.T[:, None, None, :], 0.0), axis=-1)
    return jnp.where(band[None], bias, NEG_INF)


def _attn_kernel(q_ref, kp_ref, kc_ref, kn_ref, vp_ref, vc_ref, vn_ref, bias_ref, o_ref, lse_ref,
                 kbuf_ref, vbuf_ref, *, sub_len, scale):
    i = pl.program_id(2)
    tq = ATT_TQ
    for buf, p, c, n in ((kbuf_ref, kp_ref, kc_ref, kn_ref), (vbuf_ref, vp_ref, vc_ref, vn_ref)):
        buf[0:ATT_R, :] = p[0, 0]
        buf[ATT_R:ATT_R + tq, :] = c[0, 0]
        buf[ATT_R + tq:, :] = n[0, 0]
    key_pos = i * tq - ATT_R + lax.broadcasted_iota(jnp.int32, (1, tq + 2 * ATT_R), 1)
    valid = (key_pos >= 0) & (key_pos < sub_len)
    for h in range(HEADS_PER_GROUP):
        cols = slice(h * HEAD_DIM, (h + 1) * HEAD_DIM)
        s = lax.dot_general(q_ref[0, 0, :, cols], kbuf_ref[:, cols], (((1,), (1,)), ((), ())),
                            preferred_element_type=F32) * scale
        s = jnp.where(valid, s + bias_ref[h], NEG_INF)
        m = jnp.max(s, axis=-1, keepdims=True)
        p = jnp.exp(s - m)
        l = jnp.sum(p, axis=-1, keepdims=True)
        p = p * (1.0 / l)
        o_ref[0, 0, :, cols] = jnp.dot(p.astype(BF16), vbuf_ref[:, cols], preferred_element_type=F32)
        lse_ref[0, 0, :, cols] = jnp.broadcast_to(m + jnp.log(l), (tq, HEAD_DIM))


def _attn_group(qkv, bias):
    B, dilation, L, _ = qkv.shape
    tq = ATT_TQ
    assert L % tq == 0, (L, tq)
    hb = tq // ATT_R
    n_halo = L // ATT_R

    def cur(kind):
        return pl.BlockSpec((1, 1, tq, D_GROUP), lambda b, r, i: (b, r, i, kind))

    def before(kind):
        return pl.BlockSpec((1, 1, ATT_R, D_GROUP), lambda b, r, i: (b, r, jnp.maximum(i * hb - 1, 0), kind))

    def after(kind):
        return pl.BlockSpec((1, 1, ATT_R, D_GROUP),
                            lambda b, r, i: (b, r, jnp.minimum((i + 1) * hb, n_halo - 1), kind))

    out_spec = pl.BlockSpec((1, 1, tq, D_GROUP), lambda b, r, i: (b, r, i, 0))
    out_sds = jax.ShapeDtypeStruct((B, dilation, L, D_GROUP), F32)
    return pl.pallas_call(
        functools.partial(_attn_kernel, sub_len=L, scale=HEAD_DIM ** -0.5),
        grid=(B, dilation, L // tq),
        in_specs=[cur(0), before(1), cur(1), after(1), before(2), cur(2), after(2),
                  pl.BlockSpec((HEADS_PER_GROUP, tq, tq + 2 * ATT_R), lambda b, r, i: (0, 0, 0))],
        out_specs=[out_spec, out_spec],
        out_shape=[out_sds, out_sds],
        scratch_shapes=[pltpu.VMEM((tq + 2 * ATT_R, D_GROUP), BF16), pltpu.VMEM((tq + 2 * ATT_R, D_GROUP), BF16)],
        compiler_params=_params("parallel", "parallel", "arbitrary"),
        name=f"attn_d{dilation}",
    )(qkv, qkv, qkv, qkv, qkv, qkv, qkv, bias)


def _combine_kernel(*refs):
    n = N_GROUPS
    o_refs, l_refs, out_ref, buf_ref = refs[:n], refs[n:2 * n], refs[2 * n], refs[2 * n + 1]

    def token_order(ref, slot, lanes):
        dilation, rows = ref.shape[1], ref.shape[2]
        if dilation == 1:
            return ref[0, 0, :, lanes]
        for r in range(dilation):
            buf_ref[slot, pl.ds(r, rows, stride=dilation), :] = ref[0, r, :, lanes]
        return buf_ref[slot]

    for c in range(out_ref.shape[2] // LANES):
        lanes = slice(c * LANES, (c + 1) * LANES)
        outs = [token_order(ref, g, lanes) for g, ref in enumerate(o_refs)]
        lses = [token_order(ref, n + g, lanes) for g, ref in enumerate(l_refs)]
        mx = functools.reduce(jnp.maximum, lses)
        es = [jnp.exp(l - mx) for l in lses]
        den = functools.reduce(lambda a, b: a + b, es)
        acc = None
        for e, o in zip(es, outs):
            term = (e / den) * o
            acc = term if acc is None else acc + term
        out_ref[0, :, lanes] = acc.astype(out_ref.dtype)


def _combine(outs, lses, S):
    B = outs[0].shape[0]
    N = outs[0].shape[-1]
    max_dil = max(o.shape[1] for o in outs)
    tm = _tile(S, 256, SUBLANES_F32 * max_dil)

    def spec(a):
        d = a.shape[1]
        return pl.BlockSpec((1, d, tm // d, N), lambda b, i: (b, 0, i, 0))

    return pl.pallas_call(
        _combine_kernel,
        grid=(B, S // tm),
        in_specs=[spec(a) for a in (*outs, *lses)],
        out_specs=pl.BlockSpec((1, tm, N), lambda b, i: (b, i, 0)),
        out_shape=jax.ShapeDtypeStruct((B, S, N), BF16),
        scratch_shapes=[pltpu.VMEM((2 * N_GROUPS, tm, LANES), F32)],
        compiler_params=_params("parallel", "arbitrary"),
        name="attn_combine",
    )(*outs, *lses)


def _merge_kernel(x_ref, hc_ref, at_ref, wg0_ref, wg1_ref, wc_ref, wa_ref, b0_ref, b1_ref, o_ref):
    x = x_ref[...]
    g0 = _sigmoid(jnp.dot(x, wg0_ref[...], preferred_element_type=F32) + b0_ref[...])
    g1 = _sigmoid(jnp.dot(x, wg1_ref[...], preferred_element_type=F32) + b1_ref[...])
    conv_out = jnp.dot(hc_ref[...], wc_ref[...], preferred_element_type=F32)
    attn_out = jnp.dot(at_ref[...], wa_ref[...], preferred_element_type=F32)
    o_ref[...] = (g0 * conv_out + g1 * attn_out).astype(o_ref.dtype)


def _merge(xb, hc, attn, w_in, b_in, w_conv_out, w_attn_out, col_gate):
    M, D = xb.shape
    tm = _tile(M, 1024, ROWS_BF16_TILE)
    tn = _tile(math.gcd(D, col_gate), 256)
    off0 = col_gate // tn
    off1 = (col_gate + D) // tn
    return pl.pallas_call(
        _merge_kernel,
        grid=(M // tm, D // tn),
        in_specs=[
            pl.BlockSpec((tm, D), lambda i, j: (i, 0)),
            pl.BlockSpec((tm, hc.shape[1]), lambda i, j: (i, 0)),
            pl.BlockSpec((tm, attn.shape[1]), lambda i, j: (i, 0)),
            pl.BlockSpec((D, tn), lambda i, j: (0, j + off0)),
            pl.BlockSpec((D, tn), lambda i, j: (0, j + off1)),
            pl.BlockSpec((hc.shape[1], tn), lambda i, j: (0, j)),
            pl.BlockSpec((attn.shape[1], tn), lambda i, j: (0, j)),
            pl.BlockSpec((1, tn), lambda i, j: (0, j + off0)),
            pl.BlockSpec((1, tn), lambda i, j: (0, j + off1)),
        ],
        out_specs=pl.BlockSpec((tm, tn), lambda i, j: (i, j)),
        out_shape=jax.ShapeDtypeStruct((M, D), BF16),
        compiler_params=_params("parallel", "arbitrary"),
        name="merge",
    )(xb, hc, attn, w_in, w_in, w_conv_out, w_attn_out, b_in, b_in)


def _zero_rows(ref, n_rows):
    def zero(r):
        ref[pl.ds(r, EPILOGUE_ROWS), :] = jnp.zeros((EPILOGUE_ROWS, ref.shape[1]), ref.dtype)
    _for_row_chunks(n_rows, zero)


def _deferred_norm_steps(n_tiles, acc_refs, main, norm):
    i = pl.program_id(0)
    k = pl.program_id(1)

    @pl.when((i == 0) & (k == 0))
    def _():
        for acc_ref in acc_refs:
            _zero_rows(acc_ref, acc_ref.shape[0])

    @pl.when(i == 0)
    def _():
        main(acc_refs[0])

    for parity in (0, 1):
        @pl.when((i > 0) & (i < n_tiles) & (i % 2 == parity))
        def _():
            norm(acc_refs[1 - parity])
            main(acc_refs[parity])

    @pl.when(i == n_tiles)
    def _():
        norm(acc_refs[(n_tiles - 1) % 2])


def _slab_chunks(k, slab_rows):
    for c in range(slab_rows // EPILOGUE_ROWS):
        r = c * EPILOGUE_ROWS
        yield (pl.ds(pl.multiple_of(k * slab_rows + r, EPILOGUE_ROWS), EPILOGUE_ROWS), pl.ds(r, EPILOGUE_ROWS))


def _deferred_specs(n_tiles, n_steps, tm, D):
    rows = tm // n_steps

    def main_step(i, k):
        return jnp.where(i < n_tiles, k, n_steps - 1)

    def main_tile(i):
        return jnp.minimum(i, n_tiles - 1)

    def a_spec(tk):
        return pl.BlockSpec((tm, tk), lambda i, k: (main_tile(i), main_step(i, k)))

    def w_spec(tk):
        return pl.BlockSpec((tk, D), lambda i, k: (main_step(i, k), 0))

    def main_slab(cols=D):
        return pl.BlockSpec((rows, cols), lambda i, k: (main_tile(i) * n_steps + main_step(i, k), 0))

    def norm_slab(cols=D):
        return pl.BlockSpec((rows, cols), lambda i, k: (jnp.maximum(i - 1, 0) * n_steps + jnp.where(i > 0, k, 0), 0))

    return a_spec, w_spec, main_slab, norm_slab


def _out_proj_kernel(a_ref, w_ref, x_ref, g_ref, b_ref, o32_ref, o16_ref, acc0_ref, acc1_ref, *, alpha, n_tiles):
    k = pl.program_id(1)
    slab_rows = x_ref.shape[0]

    def main(acc_ref):
        _dot_into(acc_ref, [(a_ref[...], w_ref)], accumulate=True)
        for acc_rows, rows in _slab_chunks(k, slab_rows):
            acc_ref[acc_rows, :] = acc_ref[acc_rows, :] + alpha * x_ref[rows, :]

    def norm(acc_ref):
        for acc_rows, rows in _slab_chunks(k, slab_rows):
            y = _layer_norm(acc_ref[acc_rows, :], g_ref[...], b_ref[...])
            o32_ref[rows, :] = y
            o16_ref[rows, :] = y.astype(BF16)
            acc_ref[acc_rows, :] = jnp.zeros_like(y)

    _deferred_norm_steps(n_tiles, (acc0_ref, acc1_ref), main, norm)


def _out_proj(a, w, x, g, b, alpha):
    M, K = a.shape
    D = w.shape[1]
    tk = _tile(K, 1024)
    nk = K // tk
    tm = _tile(M, 512, EPILOGUE_ROWS * nk)
    n_tiles = M // tm
    a_spec, w_spec, main_slab, norm_slab = _deferred_specs(n_tiles, nk, tm, D)
    vec = pl.BlockSpec((1, D), lambda i, k: (0, 0))
    return pl.pallas_call(
        functools.partial(_out_proj_kernel, alpha=alpha, n_tiles=n_tiles),
        grid=(n_tiles + 1, nk),
        in_specs=[a_spec(tk), w_spec(tk), main_slab(), vec, vec],
        out_specs=[norm_slab(), norm_slab()],
        out_shape=[jax.ShapeDtypeStruct((M, D), F32), jax.ShapeDtypeStruct((M, D), BF16)],
        scratch_shapes=[pltpu.VMEM((tm, D), F32), pltpu.VMEM((tm, D), F32)],
        compiler_params=_params("arbitrary", "arbitrary"),
        name="out_proj",
    )(a, w, x, g, b)


def _ple_kernel(a_ref, w_ref, x_ref, p_ref, wp_ref, bg_ref, g_ref, b_ref, o_ref, acc0_ref, acc1_ref,
                *, alpha, n_tiles):
    k = pl.program_id(1)
    slab_rows = x_ref.shape[0]

    def main(acc_ref):
        _dot_into(acc_ref, [(a_ref[...], w_ref)], accumulate=True)

    def norm(acc_ref):
        for acc_rows, rows in _slab_chunks(k, slab_rows):
            gate = _sigmoid(acc_ref[acc_rows, :] + bg_ref[...])
            pw = jnp.dot(p_ref[rows, :].astype(BF16), wp_ref[...], preferred_element_type=F32)
            o_ref[rows, :] = _layer_norm(alpha * x_ref[rows, :] + pw * gate, g_ref[...], b_ref[...])
            acc_ref[acc_rows, :] = jnp.zeros_like(gate)

    _deferred_norm_steps(n_tiles, (acc0_ref, acc1_ref), main, norm)


def _ple(a, w, x, p, wp, bg, g, b, alpha):
    M, K = a.shape
    D = w.shape[1]
    P = p.shape[1]
    tk = _tile(K, 1024)
    nk = K // tk
    tm = _tile(M, 512, EPILOGUE_ROWS * nk)
    n_tiles = M // tm
    a_spec, w_spec, _, norm_slab = _deferred_specs(n_tiles, nk, tm, D)
    vec = pl.BlockSpec((1, D), lambda i, k: (0, 0))
    return pl.pallas_call(
        functools.partial(_ple_kernel, alpha=alpha, n_tiles=n_tiles),
        grid=(n_tiles + 1, nk),
        in_specs=[a_spec(tk), w_spec(tk), norm_slab(), norm_slab(P),
                  pl.BlockSpec((P, D), lambda i, k: (0, 0)), vec, vec, vec],
        out_specs=norm_slab(),
        out_shape=jax.ShapeDtypeStruct((M, D), F32),
        scratch_shapes=[pltpu.VMEM((tm, D), F32), pltpu.VMEM((tm, D), F32)],
        compiler_params=_params("arbitrary", "arbitrary"),
        name="ple",
    )(a, w, x, p, wp, bg, g, b)


def _encoder_layer(x, p, B, S, w, biases, alpha):
    d_conv = w["conv_dw"].shape[1]
    col_q = 2 * d_conv
    col_gate = col_q + 3 * D_ATTN
    ln_g, ln_b = w["ln_g"], w["ln_b"]

    x1, x1b = _ffn(x, x.astype(BF16), w["w_ff1_gate"], w["w_ff1_up"], w["w_ff1_down"], ln_g[0:1], ln_b[0:1], alpha)

    h = _glu_proj(x1b, w["w_in"], w["b_in"], d_conv)
    hc = _conv_module(h.reshape(B, S, d_conv), w["conv_dw"], w["conv_dw_b"], w["conv_ln_g"], w["conv_ln_b"])
    hc = hc.reshape(B * S, d_conv)

    outs, lses = [], []
    for gi, (_, dil) in enumerate(DILATED_GROUPS):
        qkv = _qkv_proj(x1b, w["w_in"], w["b_in"], col_q, B, S, gi, dil)
        o, lse = _attn_group(qkv, biases[gi])
        outs.append(o)
        lses.append(lse)
    attn = _combine(outs, lses, S).reshape(B * S, D_GROUP)

    merged = _merge(x1b, hc, attn, w["w_in"], w["b_in"], w["w_conv_out"], w["w_attn_out"], col_gate)
    x2, x2b = _out_proj(merged, w["w_out"], x1, ln_g[1:2], ln_b[1:2], alpha)
    x3, x3b = _ffn(x2, x2b, w["w_ff2_gate"], w["w_ff2_up"], w["w_ff2_down"], ln_g[2:3], ln_b[2:3], alpha)
    return _ple(x3b, w["w_ple_gate"], x3, p, w["w_ple"], w["b_ple_gate"], ln_g[3:4], ln_b[3:4], alpha)


_MATMUL_WEIGHTS = ("w_ff1_gate", "w_ff1_up", "w_ff1_down", "w_in", "w_conv_out", "w_attn_out", "w_out",
                   "w_ff2_gate", "w_ff2_up", "w_ff2_down", "w_ple", "w_ple_gate")
_ROW_VECTORS = ("b_in", "conv_dw_b", "conv_ln_g", "conv_ln_b", "b_ple_gate")


def kernel(x_prompt, x_sample, p_prompt, p_sample, rel_bias, ln_g, ln_b, w_ff1_gate, w_ff1_up, w_ff1_down,
           w_in, b_in, conv_dw, conv_dw_b, conv_ln_g, conv_ln_b, w_conv_out, w_attn_out, w_out, w_ff2_gate,
           w_ff2_up, w_ff2_down, w_ple, w_ple_gate, b_ple_gate):
    stacked = dict(ln_g=ln_g, ln_b=ln_b, w_ff1_gate=w_ff1_gate, w_ff1_up=w_ff1_up, w_ff1_down=w_ff1_down,
                   w_in=w_in, b_in=b_in, conv_dw=conv_dw, conv_dw_b=conv_dw_b, conv_ln_g=conv_ln_g,
                   conv_ln_b=conv_ln_b, w_conv_out=w_conv_out, w_attn_out=w_attn_out, w_out=w_out,
                   w_ff2_gate=w_ff2_gate, w_ff2_up=w_ff2_up, w_ff2_down=w_ff2_down, w_ple=w_ple,
                   w_ple_gate=w_ple_gate, b_ple_gate=b_ple_gate)
    depth = ln_g.shape[0]
    alpha = (2.0 * depth) ** 0.25
    layers = []
    for i in range(depth):
        w = {k: v[i] for k, v in stacked.items()}
        for k in _MATMUL_WEIGHTS:
            w[k] = w[k].astype(BF16)
        for k in _ROW_VECTORS:
            w[k] = w[k].reshape(1, -1)
        layers.append(w)
    biases = [_band_bias(rel_bias, gi, dil) for gi, (_, dil) in enumerate(DILATED_GROUPS)]

    def run(x, p):
        B, S, D = x.shape
        y = x.reshape(B * S, D)
        for i, w in enumerate(layers):
            y = _encoder_layer(y, p[i].reshape(B * S, -1), B, S, w, biases, alpha)
        return y.reshape(B, S, D)

    return run(x_prompt, p_prompt), run(x_sample, p_sample)
```

```python
import functools
import math

import jax
import jax.numpy as jnp
import numpy as np
from jax import lax
from jax.experimental import pallas as pl
from jax.experimental.pallas import tpu as pltpu

F32 = jnp.float32
BF16 = jnp.bfloat16

HEAD_DIM = 128
HEADS_PER_GROUP = 8
DILATED_GROUPS = ((128, 1), (512, 4), (2048, 16))
N_GROUPS = len(DILATED_GROUPS)
D_GROUP = HEADS_PER_GROUP * HEAD_DIM
D_ATTN = N_GROUPS * D_GROUP
CONV_WIDTH = 31
NUM_BUCKETS = 32
MAX_DISTANCE = 1024
LN_EPS = 1e-5
NEG_INF = -1e30

LANES = 128
SUBLANES_F32 = 8
ROWS_BF16_TILE = 16
MAX_CLEAN_STRIDE = 4
VMEM_LIMIT_BYTES = 58 * 1024 * 1024
FFN_VMEM_LIMIT_BYTES = 63 * 1024 * 1024

ATT_R = 64
ATT_TQ = 128
ATT_STEP_QUERIES = 1024
CONV_HALO = 16
CONV_ROWS = 16
ACC_CHUNK_COLS = 512
EPILOGUE_ROWS = 64
LN_CHUNK_COLS = 512
FFN_NORM_SLABS = 16


def _tile(n, target, align=LANES):
    if n <= target:
        return n
    t = (target // align) * align
    while t >= align:
        if n % t == 0:
            return t
        t -= align
    raise ValueError(f"no tile for {n} <= {target} aligned to {align}")


def _params(*sem, vmem_limit_bytes=VMEM_LIMIT_BYTES):
    return pltpu.CompilerParams(dimension_semantics=sem, vmem_limit_bytes=vmem_limit_bytes)


def _layer_norm(r, g, b):
    mu = jnp.mean(r, axis=-1, keepdims=True)
    c = r - mu
    var = jnp.mean(c * c, axis=-1, keepdims=True)
    return c * lax.rsqrt(var + LN_EPS) * g + b


def _layer_norm_chunked(read, n_cols, g_ref, b_ref, write):
    chunks = [slice(c, c + LN_CHUNK_COLS) for c in range(0, n_cols, LN_CHUNK_COLS)]
    total = functools.reduce(lambda a, b: a + b, [read(cols) for cols in chunks])
    mu = jnp.sum(total, axis=-1, keepdims=True) / n_cols
    sq = None
    for cols in chunks:
        d = read(cols) - mu
        sq = d * d if sq is None else sq + d * d
    rstd = lax.rsqrt(jnp.sum(sq, axis=-1, keepdims=True) / n_cols + LN_EPS)
    for cols in chunks:
        write(cols, (read(cols) - mu) * rstd * g_ref[:, cols] + b_ref[:, cols])


def _sigmoid(x):
    return 1.0 / (1.0 + jnp.exp(-x))


def _for_row_chunks(n_rows, fn):
    def body(c, carry):
        fn(pl.multiple_of(c * EPILOGUE_ROWS, EPILOGUE_ROWS))
        return carry
    lax.fori_loop(0, n_rows // EPILOGUE_ROWS, body, 0)


def _dot_into(acc_ref, terms, accumulate):
    n = acc_ref.shape[1]
    tn = _tile(n, ACC_CHUNK_COLS)
    for c in range(n // tn):
        cols = slice(c * tn, (c + 1) * tn)
        prod = None
        for a, w_ref in terms:
            term = jnp.dot(a, w_ref[:, cols], preferred_element_type=F32)
            prod = term if prod is None else prod + term
        acc_ref[:, cols] = acc_ref[:, cols] + prod if accumulate else prod


def _residual_steps(step, n_slabs, acc_ref, x_ref, scale):
    slab_rows = x_ref.shape[0]

    @pl.when(step < n_slabs)
    def _():
        slab0 = step * slab_rows

        def add(r):
            rows = pl.ds(pl.multiple_of(slab0 + r, EPILOGUE_ROWS), EPILOGUE_ROWS)
            acc_ref[rows, :] = acc_ref[rows, :] + scale * x_ref[pl.ds(r, EPILOGUE_ROWS), :]
        _for_row_chunks(slab_rows, add)


def _norm_steps(step, n_main, slab_rows, fn):
    @pl.when(step >= n_main)
    def _():
        slab0 = (step - n_main) * slab_rows
        _for_row_chunks(slab_rows, lambda r: fn(pl.ds(pl.multiple_of(slab0 + r, EPILOGUE_ROWS), EPILOGUE_ROWS),
                                                pl.ds(r, EPILOGUE_ROWS)))


def _slab_spec_factory(tm, width, n_main, n_slabs):
    rows = tm // n_slabs

    def late(cols=width):
        return pl.BlockSpec((rows, cols), lambda i, s: (i * n_slabs + jnp.clip(s - n_main, 0, n_slabs - 1), 0))

    def early(cols=width):
        return pl.BlockSpec((rows, cols), lambda i, s: (i * n_slabs + jnp.minimum(s, n_slabs - 1), 0))
    return late, early


def _ffn_kernel(xb_ref, wg0_ref, wg1_ref, wu0_ref, wu1_ref, wd0_ref, wd1_ref, x_ref, g_ref, b_ref,
                o32_ref, o16_ref, acc_ref, *, alpha, nf, n_slabs):
    s = pl.program_id(1)
    n_main = (nf + 1) // 2

    def terms(n_chunks):
        xb = xb_ref[...]
        out = []
        for wg_ref, wu_ref, wd_ref in ((wg0_ref, wu0_ref, wd0_ref), (wg1_ref, wu1_ref, wd1_ref))[:n_chunks]:
            gt = jnp.dot(xb, wg_ref[...], preferred_element_type=F32)
            up = jnp.dot(xb, wu_ref[...], preferred_element_type=F32)
            out.append((((gt * _sigmoid(gt)) * up).astype(BF16), wd_ref))
        return out

    n_full = nf // 2

    @pl.when(s == 0)
    def _():
        _dot_into(acc_ref, terms(2 if n_full > 0 else 1), accumulate=False)

    @pl.when((s > 0) & (s < n_full))
    def _():
        _dot_into(acc_ref, terms(2), accumulate=True)

    if nf % 2 and n_full > 0:
        @pl.when(s == n_full)
        def _():
            _dot_into(acc_ref, terms(1), accumulate=True)

    _residual_steps(s, n_slabs, acc_ref, x_ref, 2.0 * alpha)

    def finish(acc_rows, rows):
        def write(cols, y):
            o32_ref[rows, cols] = y
            o16_ref[rows, cols] = y.astype(BF16)
        _layer_norm_chunked(lambda cols: 0.5 * acc_ref[acc_rows, cols], acc_ref.shape[1], g_ref, b_ref, write)

    _norm_steps(s, n_main, o32_ref.shape[0], finish)


def _ffn(x, xb, wg, wu, wd, g, b, alpha):
    M, D = x.shape
    F = wg.shape[1]
    tf = _tile(F, 256)
    nf = F // tf
    n_main = (nf + 1) // 2
    n_slabs = min(FFN_NORM_SLABS, n_main)
    tm = _tile(M, 1024, EPILOGUE_ROWS * n_slabs)
    late, early = _slab_spec_factory(tm, D, n_main, n_slabs)
    last = nf - 1
    vec = pl.BlockSpec((1, D), lambda i, s: (0, 0))

    def cols(which):
        return pl.BlockSpec((D, tf), lambda i, s: (0, jnp.minimum(2 * s + which, last)))

    def rows(which):
        return pl.BlockSpec((tf, D), lambda i, s: (jnp.minimum(2 * s + which, last), 0))

    return pl.pallas_call(
        functools.partial(_ffn_kernel, alpha=alpha, nf=nf, n_slabs=n_slabs),
        grid=(M // tm, n_main + n_slabs),
        in_specs=[
            pl.BlockSpec((tm, D), lambda i, s: (i, 0), pipeline_mode=pl.Buffered(1)),
            cols(0), cols(1), cols(0), cols(1), rows(0), rows(1),
            early(), vec, vec,
        ],
        out_specs=[late(), late()],
        out_shape=[jax.ShapeDtypeStruct((M, D), F32), jax.ShapeDtypeStruct((M, D), BF16)],
        scratch_shapes=[pltpu.VMEM((tm, D), F32)],
        compiler_params=_params("parallel", "arbitrary", vmem_limit_bytes=FFN_VMEM_LIMIT_BYTES),
        name="ffn",
    )(xb, wg, wg, wu, wu, wd, wd, x, g, b)


def _glu_proj_kernel(x_ref, wa_ref, wg_ref, ba_ref, bg_ref, o_ref):
    x = x_ref[...]
    a = jnp.dot(x, wa_ref[...], preferred_element_type=F32) + ba_ref[...]
    gt = jnp.dot(x, wg_ref[...], preferred_element_type=F32) + bg_ref[...]
    o_ref[...] = a * _sigmoid(gt)


def _glu_proj(xb, w_in, b_in, d_conv):
    M, D = xb.shape
    tm = _tile(M, 1024, ROWS_BF16_TILE)
    tn = _tile(d_conv, 512)
    off = d_conv // tn
    return pl.pallas_call(
        _glu_proj_kernel,
        grid=(M // tm, d_conv // tn),
        in_specs=[
            pl.BlockSpec((tm, D), lambda i, j: (i, 0)),
            pl.BlockSpec((D, tn), lambda i, j: (0, j)),
            pl.BlockSpec((D, tn), lambda i, j: (0, j + off)),
            pl.BlockSpec((1, tn), lambda i, j: (0, j)),
            pl.BlockSpec((1, tn), lambda i, j: (0, j + off)),
        ],
        out_specs=pl.BlockSpec((tm, tn), lambda i, j: (i, j)),
        out_shape=jax.ShapeDtypeStruct((M, d_conv), F32),
        compiler_params=_params("parallel", "arbitrary"),
        name="glu_proj",
    )(xb, w_in, w_in, b_in, b_in)


def _qkv_proj_kernel(x_ref, w_ref, b_ref, o_ref, *scratch, dilation):
    res = jnp.dot(x_ref[...], w_ref[...], preferred_element_type=F32) + b_ref[...]
    if dilation == 1:
        o_ref[0, 0] = res.astype(o_ref.dtype)
    else:
        res_ref, tmp_ref = scratch
        tile_rows = res_ref.shape[1]
        d1 = tmp_ref.shape[0]
        d2 = dilation // d1
        for c in range(res_ref.shape[0]):
            lanes = slice(c * LANES, (c + 1) * LANES)
            res_ref[c] = res[:, lanes]
            for r1 in range(d1):
                if d2 == 1:
                    o_ref[0, r1, :, lanes] = res_ref[c, pl.ds(r1, tile_rows // d1, stride=d1), :].astype(o_ref.dtype)
                else:
                    tmp_ref[r1] = res_ref[c, pl.ds(r1, tile_rows // d1, stride=d1), :]
            if d2 > 1:
                for r1 in range(d1):
                    for r2 in range(d2):
                        o_ref[0, r2 * d1 + r1, :, lanes] = (
                            tmp_ref[r1, pl.ds(r2, tile_rows // dilation, stride=d2), :].astype(o_ref.dtype))


def _qkv_proj(xb, w_in, b_in, col_q, B, S, group, dilation):
    M, D = xb.shape
    L = S // dilation
    tm = _tile(S, 1024, ROWS_BF16_TILE * dilation)
    tn = _tile(math.gcd(D_GROUP, col_q), 1024)
    per_kind = D_GROUP // tn
    tiles_per_seq = S // tm
    d1 = min(dilation, MAX_CLEAN_STRIDE)
    assert dilation % d1 == 0 and dilation // d1 <= MAX_CLEAN_STRIDE, dilation

    def w_col(i, j):
        kind = j // per_kind
        return (0, (col_q + kind * D_ATTN + group * D_GROUP) // tn + j % per_kind)

    return pl.pallas_call(
        functools.partial(_qkv_proj_kernel, dilation=dilation),
        grid=(M // tm, 3 * per_kind),
        in_specs=[
            pl.BlockSpec((tm, D), lambda i, j: (i, 0)),
            pl.BlockSpec((D, tn), w_col),
            pl.BlockSpec((1, tn), w_col),
        ],
        out_specs=pl.BlockSpec((1, dilation, tm // dilation, tn),
                               lambda i, j: (i // tiles_per_seq, 0, i % tiles_per_seq, j)),
        out_shape=jax.ShapeDtypeStruct((B, dilation, L, 3 * D_GROUP), BF16),
        scratch_shapes=[] if dilation == 1 else [pltpu.VMEM((tn // LANES, tm, LANES), F32),
                                                 pltpu.VMEM((d1, tm // d1, LANES), F32)],
        compiler_params=_params("parallel", "arbitrary"),
        name=f"qkv_proj_d{dilation}",
    )(xb, w_in, b_in)


def _conv_kernel(prev_ref, cur_ref, next_ref, dw_ref, dwb_ref, g_ref, b_ref, o_ref, buf_ref, shift_ref,
                 *, ts, n_tiles):
    i = pl.program_id(1)
    buf_ref[0:CONV_HALO, :] = jnp.where(i > 0, prev_ref[0], 0.0)
    buf_ref[CONV_HALO:CONV_HALO + ts, :] = cur_ref[0]
    buf_ref[CONV_HALO + ts:, :] = jnp.where(i < n_tiles - 1, next_ref[0], 0.0)
    n_shift = shift_ref.shape[1]
    for s in range(1, SUBLANES_F32):
        shift_ref[s - 1] = buf_ref[s:s + n_shift, :]
    first = CONV_HALO - CONV_WIDTH // 2
    for c in range(ts // CONV_ROWS):
        acc = None
        for w in range(CONV_WIDTH):
            row = first + c * CONV_ROWS + w
            s, base = row % SUBLANES_F32, row - row % SUBLANES_F32
            src = buf_ref if s == 0 else shift_ref.at[s - 1]
            tap = jnp.concatenate([dw_ref[w]] * (CONV_ROWS // SUBLANES_F32), axis=0)
            term = src[base:base + CONV_ROWS, :] * tap
            acc = term if acc is None else acc + term
        y = _layer_norm(acc + dwb_ref[...], g_ref[...], b_ref[...])
        o_ref[0, c * CONV_ROWS:(c + 1) * CONV_ROWS, :] = (y * _sigmoid(y)).astype(o_ref.dtype)


def _conv_module(h, dw, dwb, g, b):
    B, S, C = h.shape
    ts = _tile(S, 128, CONV_HALO)
    n_tiles = S // ts
    hb = ts // CONV_HALO
    n_halo = S // CONV_HALO
    row = lambda bi, i: (0, 0)
    n_buf = ts + 2 * CONV_HALO
    taps = jnp.broadcast_to(dw[:, None, :], (CONV_WIDTH, SUBLANES_F32, C))
    return pl.pallas_call(
        functools.partial(_conv_kernel, ts=ts, n_tiles=n_tiles),
        grid=(B, n_tiles),
        in_specs=[
            pl.BlockSpec((1, CONV_HALO, C), lambda bi, i: (bi, jnp.maximum(i * hb - 1, 0), 0)),
            pl.BlockSpec((1, ts, C), lambda bi, i: (bi, i, 0)),
            pl.BlockSpec((1, CONV_HALO, C), lambda bi, i: (bi, jnp.minimum((i + 1) * hb, n_halo - 1), 0)),
            pl.BlockSpec((CONV_WIDTH, SUBLANES_F32, C), lambda bi, i: (0, 0, 0)),
            pl.BlockSpec((1, C), row),
            pl.BlockSpec((1, C), row),
            pl.BlockSpec((1, C), row),
        ],
        out_specs=pl.BlockSpec((1, ts, C), lambda bi, i: (bi, i, 0)),
        out_shape=jax.ShapeDtypeStruct((B, S, C), BF16),
        scratch_shapes=[pltpu.VMEM((n_buf, C), F32),
                        pltpu.VMEM((SUBLANES_F32 - 1, n_buf - SUBLANES_F32, C), F32)],
        compiler_params=_params("parallel", "arbitrary"),
        name="conv_module",
    )(h, h, h, taps, dwb, g, b)


def _t5_bucket(rel):
    half = NUM_BUCKETS // 2
    max_exact = half // 2
    ret = (rel > 0).astype(np.int32) * half
    n = np.abs(rel)
    large = max_exact + (np.log(np.maximum(n, 1) / max_exact) / np.log(MAX_DISTANCE / max_exact)
                         * (half - max_exact)).astype(np.int32)
    large = np.minimum(large, half - 1)
    return (ret + np.where(n < max_exact, n, large)).astype(np.int32)


def _band_bias(rel_bias, group, dilation):
    qi = np.arange(ATT_TQ)[:, None]
    kj = np.arange(ATT_TQ + 2 * ATT_R)[None, :]
    rel = kj - ATT_R - qi
    band = np.abs(rel) <= ATT_R
    bucket = _t5_bucket(np.clip(rel, -ATT_R, ATT_R) * dilation)
    table = rel_bias[:, group * HEADS_PER_GROUP:(group + 1) * HEADS_PER_GROUP].astype(F32)
    onehot = bucket[None, :, :, None] == np.arange(NUM_BUCKETS)
    bias = jnp.sum(jnp.where(onehot, table.T[:, None, None, :], 0.0), axis=-1)
    return jnp.where(band[None], bias, NEG_INF)


def _attn_kernel(q_ref, kp_ref, kc_ref, kn_ref, vp_ref, vc_ref, vn_ref, bias_ref, o_ref, lse_ref,
                 kbuf_ref, vbuf_ref, *, sub_len, scale):
    i = pl.program_id(2)
    n_res, tq = q_ref.shape[1], q_ref.shape[2]
    n_keys = ATT_TQ + 2 * ATT_R
    for r in range(n_res):
        for buf, p, c, n in ((kbuf_ref, kp_ref, kc_ref, kn_ref), (vbuf_ref, vp_ref, vc_ref, vn_ref)):
            buf[r, 0:ATT_R, :] = p[0, r]
            buf[r, ATT_R:ATT_R + tq, :] = c[0, r]
            buf[r, ATT_R + tq:, :] = n[0, r]
    for r in range(n_res):
        for sub in range(tq // ATT_TQ):
            q_rows = slice(sub * ATT_TQ, (sub + 1) * ATT_TQ)
            k_rows = slice(sub * ATT_TQ, sub * ATT_TQ + n_keys)
            key_pos = i * tq + sub * ATT_TQ - ATT_R + lax.broadcasted_iota(jnp.int32, (1, n_keys), 1)
            valid = (key_pos >= 0) & (key_pos < sub_len)
            for h in range(HEADS_PER_GROUP):
                cols = slice(h * HEAD_DIM, (h + 1) * HEAD_DIM)
                s = lax.dot_general(q_ref[0, r, q_rows, cols], kbuf_ref[r, k_rows, cols], (((1,), (1,)), ((), ())),
                                    preferred_element_type=F32) * scale
                s = jnp.where(valid, s + bias_ref[h], NEG_INF)
                m = jnp.max(s, axis=-1, keepdims=True)
                p = jnp.exp(s - m)
                l = jnp.sum(p, axis=-1, keepdims=True)
                p = p * (1.0 / l)
                o_ref[0, r, q_rows, cols] = jnp.dot(p.astype(BF16), vbuf_ref[r, k_rows, cols],
                                                    preferred_element_type=F32)
                lse_ref[0, r, q_rows, cols] = jnp.broadcast_to(m + jnp.log(l), (ATT_TQ, HEAD_DIM))


def _attn_group(qkv, bias):
    B, dilation, L, _ = qkv.shape
    assert L % ATT_TQ == 0, (L, ATT_TQ)
    tq = _tile(L, ATT_STEP_QUERIES, ATT_TQ)
    n_res = _tile(dilation, max(ATT_STEP_QUERIES // tq, 1), 1)
    hb = tq // ATT_R
    n_halo = L // ATT_R

    def cur(kind):
        return pl.BlockSpec((1, n_res, tq, D_GROUP), lambda b, r, i: (b, r, i, kind))

    def before(kind):
        return pl.BlockSpec((1, n_res, ATT_R, D_GROUP), lambda b, r, i: (b, r, jnp.maximum(i * hb - 1, 0), kind))

    def after(kind):
        return pl.BlockSpec((1, n_res, ATT_R, D_GROUP),
                            lambda b, r, i: (b, r, jnp.minimum((i + 1) * hb, n_halo - 1), kind))

    out_spec = pl.BlockSpec((1, n_res, tq, D_GROUP), lambda b, r, i: (b, r, i, 0))
    out_sds = jax.ShapeDtypeStruct((B, dilation, L, D_GROUP), F32)
    buf = pltpu.VMEM((n_res, tq + 2 * ATT_R, D_GROUP), BF16)
    return pl.pallas_call(
        functools.partial(_attn_kernel, sub_len=L, scale=HEAD_DIM ** -0.5),
        grid=(B, dilation // n_res, L // tq),
        in_specs=[cur(0), before(1), cur(1), after(1), before(2), cur(2), after(2),
                  pl.BlockSpec((HEADS_PER_GROUP, ATT_TQ, ATT_TQ + 2 * ATT_R), lambda b, r, i: (0, 0, 0))],
        out_specs=[out_spec, out_spec],
        out_shape=[out_sds, out_sds],
        scratch_shapes=[buf, buf],
        compiler_params=_params("parallel", "parallel", "arbitrary"),
        name=f"attn_d{dilation}",
    )(qkv, qkv, qkv, qkv, qkv, qkv, qkv, bias)


def _combine_kernel(*refs):
    n = N_GROUPS
    o_refs, l_refs, out_ref, buf_ref = refs[:n], refs[n:2 * n], refs[2 * n], refs[2 * n + 1]

    def token_order(ref, slot, lanes):
        dilation, rows = ref.shape[1], ref.shape[2]
        if dilation == 1:
            return ref[0, 0, :, lanes]
        for r in range(dilation):
            buf_ref[slot, pl.ds(r, rows, stride=dilation), :] = ref[0, r, :, lanes]
        return buf_ref[slot]

    for c in range(out_ref.shape[2] // LANES):
        lanes = slice(c * LANES, (c + 1) * LANES)
        outs = [token_order(ref, g, lanes) for g, ref in enumerate(o_refs)]
        lses = [token_order(ref, n + g, lanes) for g, ref in enumerate(l_refs)]
        mx = functools.reduce(jnp.maximum, lses)
        es = [jnp.exp(l - mx) for l in lses]
        den = functools.reduce(lambda a, b: a + b, es)
        acc = None
        for e, o in zip(es, outs):
            term = (e / den) * o
            acc = term if acc is None else acc + term
        out_ref[0, :, lanes] = acc.astype(out_ref.dtype)


def _combine(outs, lses, S):
    B = outs[0].shape[0]
    N = outs[0].shape[-1]
    max_dil = max(o.shape[1] for o in outs)
    tm = _tile(S, 256, SUBLANES_F32 * max_dil)

    def spec(a):
        d = a.shape[1]
        return pl.BlockSpec((1, d, tm // d, N), lambda b, i: (b, 0, i, 0))

    return pl.pallas_call(
        _combine_kernel,
        grid=(B, S // tm),
        in_specs=[spec(a) for a in (*outs, *lses)],
        out_specs=pl.BlockSpec((1, tm, N), lambda b, i: (b, i, 0)),
        out_shape=jax.ShapeDtypeStruct((B, S, N), BF16),
        scratch_shapes=[pltpu.VMEM((2 * N_GROUPS, tm, LANES), F32)],
        compiler_params=_params("parallel", "arbitrary"),
        name="attn_combine",
    )(*outs, *lses)


def _merge_kernel(x_ref, hc_ref, at_ref, wg0_ref, wg1_ref, wc_ref, wa_ref, b0_ref, b1_ref, o_ref):
    x = x_ref[...]
    g0 = _sigmoid(jnp.dot(x, wg0_ref[...], preferred_element_type=F32) + b0_ref[...])
    g1 = _sigmoid(jnp.dot(x, wg1_ref[...], preferred_element_type=F32) + b1_ref[...])
    conv_out = jnp.dot(hc_ref[...], wc_ref[...], preferred_element_type=F32)
    attn_out = jnp.dot(at_ref[...], wa_ref[...], preferred_element_type=F32)
    o_ref[...] = (g0 * conv_out + g1 * attn_out).astype(o_ref.dtype)


def _merge(xb, hc, attn, w_in, b_in, w_conv_out, w_attn_out, col_gate):
    M, D = xb.shape
    tm = _tile(M, 1024, ROWS_BF16_TILE)
    tn = _tile(math.gcd(D, col_gate), 256)
    off0 = col_gate // tn
    off1 = (col_gate + D) // tn
    return pl.pallas_call(
        _merge_kernel,
        grid=(M // tm, D // tn),
        in_specs=[
            pl.BlockSpec((tm, D), lambda i, j: (i, 0)),
            pl.BlockSpec((tm, hc.shape[1]), lambda i, j: (i, 0)),
            pl.BlockSpec((tm, attn.shape[1]), lambda i, j: (i, 0)),
            pl.BlockSpec((D, tn), lambda i, j: (0, j + off0)),
            pl.BlockSpec((D, tn), lambda i, j: (0, j + off1)),
            pl.BlockSpec((hc.shape[1], tn), lambda i, j: (0, j)),
            pl.BlockSpec((attn.shape[1], tn), lambda i, j: (0, j)),
            pl.BlockSpec((1, tn), lambda i, j: (0, j + off0)),
            pl.BlockSpec((1, tn), lambda i, j: (0, j + off1)),
        ],
        out_specs=pl.BlockSpec((tm, tn), lambda i, j: (i, j)),
        out_shape=jax.ShapeDtypeStruct((M, D), BF16),
        compiler_params=_params("parallel", "arbitrary"),
        name="merge",
    )(xb, hc, attn, w_in, w_in, w_conv_out, w_attn_out, b_in, b_in)


def _zero_rows(ref, n_rows):
    def zero(r):
        ref[pl.ds(r, EPILOGUE_ROWS), :] = jnp.zeros((EPILOGUE_ROWS, ref.shape[1]), ref.dtype)
    _for_row_chunks(n_rows, zero)


def _deferred_norm_steps(n_tiles, acc_refs, main, norm):
    i = pl.program_id(0)
    k = pl.program_id(1)

    @pl.when((i == 0) & (k == 0))
    def _():
        for acc_ref in acc_refs:
            _zero_rows(acc_ref, acc_ref.shape[0])

    @pl.when(i == 0)
    def _():
        main(acc_refs[0])

    for parity in (0, 1):
        @pl.when((i > 0) & (i < n_tiles) & (i % 2 == parity))
        def _():
            norm(acc_refs[1 - parity])
            main(acc_refs[parity])

    @pl.when(i == n_tiles)
    def _():
        norm(acc_refs[(n_tiles - 1) % 2])


def _slab_chunks(k, slab_rows):
    for c in range(slab_rows // EPILOGUE_ROWS):
        r = c * EPILOGUE_ROWS
        yield (pl.ds(pl.multiple_of(k * slab_rows + r, EPILOGUE_ROWS), EPILOGUE_ROWS), pl.ds(r, EPILOGUE_ROWS))


def _deferred_specs(n_tiles, n_steps, tm, D):
    rows = tm // n_steps

    def main_step(i, k):
        return jnp.where(i < n_tiles, k, n_steps - 1)

    def main_tile(i):
        return jnp.minimum(i, n_tiles - 1)

    def a_spec(tk):
        return pl.BlockSpec((tm, tk), lambda i, k: (main_tile(i), main_step(i, k)))

    def w_spec(tk):
        return pl.BlockSpec((tk, D), lambda i, k: (main_step(i, k), 0))

    def main_slab(cols=D):
        return pl.BlockSpec((rows, cols), lambda i, k: (main_tile(i) * n_steps + main_step(i, k), 0))

    def norm_slab(cols=D):
        return pl.BlockSpec((rows, cols), lambda i, k: (jnp.maximum(i - 1, 0) * n_steps + jnp.where(i > 0, k, 0), 0))

    return a_spec, w_spec, main_slab, norm_slab


def _out_proj_kernel(a_ref, w_ref, x_ref, g_ref, b_ref, o32_ref, o16_ref, acc0_ref, acc1_ref, *, alpha, n_tiles):
    k = pl.program_id(1)
    slab_rows = x_ref.shape[0]

    def main(acc_ref):
        _dot_into(acc_ref, [(a_ref[...], w_ref)], accumulate=True)
        for acc_rows, rows in _slab_chunks(k, slab_rows):
            acc_ref[acc_rows, :] = acc_ref[acc_rows, :] + alpha * x_ref[rows, :]

    def norm(acc_ref):
        for acc_rows, rows in _slab_chunks(k, slab_rows):
            y = _layer_norm(acc_ref[acc_rows, :], g_ref[...], b_ref[...])
            o32_ref[rows, :] = y
            o16_ref[rows, :] = y.astype(BF16)
            acc_ref[acc_rows, :] = jnp.zeros_like(y)

    _deferred_norm_steps(n_tiles, (acc0_ref, acc1_ref), main, norm)


def _out_proj(a, w, x, g, b, alpha):
    M, K = a.shape
    D = w.shape[1]
    tk = _tile(K, 1024)
    nk = K // tk
    tm = _tile(M, 512, EPILOGUE_ROWS * nk)
    n_tiles = M // tm
    a_spec, w_spec, main_slab, norm_slab = _deferred_specs(n_tiles, nk, tm, D)
    vec = pl.BlockSpec((1, D), lambda i, k: (0, 0))
    return pl.pallas_call(
        functools.partial(_out_proj_kernel, alpha=alpha, n_tiles=n_tiles),
        grid=(n_tiles + 1, nk),
        in_specs=[a_spec(tk), w_spec(tk), main_slab(), vec, vec],
        out_specs=[norm_slab(), norm_slab()],
        out_shape=[jax.ShapeDtypeStruct((M, D), F32), jax.ShapeDtypeStruct((M, D), BF16)],
        scratch_shapes=[pltpu.VMEM((tm, D), F32), pltpu.VMEM((tm, D), F32)],
        compiler_params=_params("arbitrary", "arbitrary"),
        name="out_proj",
    )(a, w, x, g, b)


def _ple_kernel(a_ref, w_ref, x_ref, p_ref, wp_ref, bg_ref, g_ref, b_ref, o_ref, acc0_ref, acc1_ref,
                *, alpha, n_tiles):
    k = pl.program_id(1)
    slab_rows = x_ref.shape[0]

    def main(acc_ref):
        _dot_into(acc_ref, [(a_ref[...], w_ref)], accumulate=True)

    def norm(acc_ref):
        for acc_rows, rows in _slab_chunks(k, slab_rows):
            gate = _sigmoid(acc_ref[acc_rows, :] + bg_ref[...])
            pw = jnp.dot(p_ref[rows, :].astype(BF16), wp_ref[...], preferred_element_type=F32)
            o_ref[rows, :] = _layer_norm(alpha * x_ref[rows, :] + pw * gate, g_ref[...], b_ref[...])
            acc_ref[acc_rows, :] = jnp.zeros_like(gate)

    _deferred_norm_steps(n_tiles, (acc0_ref, acc1_ref), main, norm)


def _ple(a, w, x, p, wp, bg, g, b, alpha):
    M, K = a.shape
    D = w.shape[1]
    P = p.shape[1]
    tk = _tile(K, 1024)
    nk = K // tk
    tm = _tile(M, 512, EPILOGUE_ROWS * nk)
    n_tiles = M // tm
    a_spec, w_spec, _, norm_slab = _deferred_specs(n_tiles, nk, tm, D)
    vec = pl.BlockSpec((1, D), lambda i, k: (0, 0))
    return pl.pallas_call(
        functools.partial(_ple_kernel, alpha=alpha, n_tiles=n_tiles),
        grid=(n_tiles + 1, nk),
        in_specs=[a_spec(tk), w_spec(tk), norm_slab(), norm_slab(P),
                  pl.BlockSpec((P, D), lambda i, k: (0, 0)), vec, vec, vec],
        out_specs=norm_slab(),
        out_shape=jax.ShapeDtypeStruct((M, D), F32),
        scratch_shapes=[pltpu.VMEM((tm, D), F32), pltpu.VMEM((tm, D), F32)],
        compiler_params=_params("arbitrary", "arbitrary"),
        name="ple",
    )(a, w, x, p, wp, bg, g, b)


def _encoder_layer(x, p, B, S, w, biases, alpha):
    d_conv = w["conv_dw"].shape[1]
    col_q = 2 * d_conv
    col_gate = col_q + 3 * D_ATTN
    ln_g, ln_b = w["ln_g"], w["ln_b"]

    x1, x1b = _ffn(x, x.astype(BF16), w["w_ff1_gate"], w["w_ff1_up"], w["w_ff1_down"], ln_g[0:1], ln_b[0:1], alpha)

    h = _glu_proj(x1b, w["w_in"], w["b_in"], d_conv)
    hc = _conv_module(h.reshape(B, S, d_conv), w["conv_dw"], w["conv_dw_b"], w["conv_ln_g"], w["conv_ln_b"])
    hc = hc.reshape(B * S, d_conv)

    outs, lses = [], []
    for gi, (_, dil) in enumerate(DILATED_GROUPS):
        qkv = _qkv_proj(x1b, w["w_in"], w["b_in"], col_q, B, S, gi, dil)
        o, lse = _attn_group(qkv, biases[gi])
        outs.append(o)
        lses.append(lse)
    attn = _combine(outs, lses, S).reshape(B * S, D_GROUP)

    merged = _merge(x1b, hc, attn, w["w_in"], w["b_in"], w["w_conv_out"], w["w_attn_out"], col_gate)
    x2, x2b = _out_proj(merged, w["w_out"], x1, ln_g[1:2], ln_b[1:2], alpha)
    x3, x3b = _ffn(x2, x2b, w["w_ff2_gate"], w["w_ff2_up"], w["w_ff2_down"], ln_g[2:3], ln_b[2:3], alpha)
    return _ple(x3b, w["w_ple_gate"], x3, p, w["w_ple"], w["b_ple_gate"], ln_g[3:4], ln_b[3:4], alpha)


_MATMUL_WEIGHTS = ("w_ff1_gate", "w_ff1_up", "w_ff1_down", "w_in", "w_conv_out", "w_attn_out", "w_out",
                   "w_ff2_gate", "w_ff2_up", "w_ff2_down", "w_ple", "w_ple_gate")
_ROW_VECTORS = ("b_in", "conv_dw_b", "conv_ln_g", "conv_ln_b", "b_ple_gate")


def kernel(x_prompt, x_sample, p_prompt, p_sample, rel_bias, ln_g, ln_b, w_ff1_gate, w_ff1_up, w_ff1_down,
           w_in, b_in, conv_dw, conv_dw_b, conv_ln_g, conv_ln_b, w_conv_out, w_attn_out, w_out, w_ff2_gate,
           w_ff2_up, w_ff2_down, w_ple, w_ple_gate, b_ple_gate):
    stacked = dict(ln_g=ln_g, ln_b=ln_b, w_ff1_gate=w_ff1_gate, w_ff1_up=w_ff1_up, w_ff1_down=w_ff1_down,
                   w_in=w_in, b_in=b_in, conv_dw=conv_dw, conv_dw_b=conv_dw_b, conv_ln_g=conv_ln_g,
                   conv_ln_b=conv_ln_b, w_conv_out=w_conv_out, w_attn_out=w_attn_out, w_out=w_out,
                   w_ff2_gate=w_ff2_gate, w_ff2_up=w_ff2_up, w_ff2_down=w_ff2_down, w_ple=w_ple,
                   w_ple_gate=w_ple_gate, b_ple_gate=b_ple_gate)
    depth = ln_g.shape[0]
    alpha = (2.0 * depth) ** 0.25
    layers = []
    for i in range(depth):
        w = {k: v[i] for k, v in stacked.items()}
        for k in _MATMUL_WEIGHTS:
            w[k] = w[k].astype(BF16)
        for k in _ROW_VECTORS:
            w[k] = w[k].reshape(1, -1)
        layers.append(w)
    biases = [_band_bias(rel_bias, gi, dil) for gi, (_, dil) in enumerate(DILATED_GROUPS)]

    def run(x, p):
        B, S, D = x.shape
        y = x.reshape(B * S, D)
        for i, w in enumerate(layers):
            y = _encoder_layer(y, p[i].reshape(B * S, -1), B, S, w, biases, alpha)
        return y.reshape(B, S, D)

    return run(x_prompt, p_prompt), run(x_sample, p_sample)
```

```python
import functools
import math

import jax
import jax.numpy as jnp
import numpy as np
from jax import lax
from jax.experimental import pallas as pl
from jax.experimental.pallas import tpu as pltpu

F32 = jnp.float32
BF16 = jnp.bfloat16

HEAD_DIM = 128
HEADS_PER_GROUP = 8
DILATED_GROUPS = ((128, 1), (512, 4), (2048, 16))
N_GROUPS = len(DILATED_GROUPS)
D_GROUP = HEADS_PER_GROUP * HEAD_DIM
D_ATTN = N_GROUPS * D_GROUP
CONV_WIDTH = 31
NUM_BUCKETS = 32
MAX_DISTANCE = 1024
LN_EPS = 1e-5
NEG_INF = -1e30

LANES = 128
SUBLANES_F32 = 8
ROWS_BF16_TILE = 16
MAX_CLEAN_STRIDE = 4
VMEM_LIMIT_BYTES = 58 * 1024 * 1024
FFN_VMEM_LIMIT_BYTES = 63 * 1024 * 1024

ATT_R = 64
ATT_TQ = 128
ATT_STEP_QUERIES = 1024
CONV_HALO = 16
CONV_ROWS = 16
ACC_CHUNK_COLS = 512
EPILOGUE_ROWS = 64
LN_CHUNK_COLS = 512
FFN_NORM_SLABS = 16


def _tile(n, target, align=LANES):
    if n <= target:
        return n
    t = (target // align) * align
    while t >= align:
        if n % t == 0:
            return t
        t -= align
    raise ValueError(f"no tile for {n} <= {target} aligned to {align}")


def _params(*sem, vmem_limit_bytes=VMEM_LIMIT_BYTES):
    return pltpu.CompilerParams(dimension_semantics=sem, vmem_limit_bytes=vmem_limit_bytes)


def _layer_norm(r, g, b):
    mu = jnp.mean(r, axis=-1, keepdims=True)
    c = r - mu
    var = jnp.mean(c * c, axis=-1, keepdims=True)
    return c * lax.rsqrt(var + LN_EPS) * g + b


def _layer_norm_chunked(read, n_cols, g_ref, b_ref, write):
    chunks = [slice(c, c + LN_CHUNK_COLS) for c in range(0, n_cols, LN_CHUNK_COLS)]
    total = functools.reduce(lambda a, b: a + b, [read(cols) for cols in chunks])
    mu = jnp.sum(total, axis=-1, keepdims=True) / n_cols
    sq = None
    for cols in chunks:
        d = read(cols) - mu
        sq = d * d if sq is None else sq + d * d
    rstd = lax.rsqrt(jnp.sum(sq, axis=-1, keepdims=True) / n_cols + LN_EPS)
    for cols in chunks:
        write(cols, (read(cols) - mu) * rstd * g_ref[:, cols] + b_ref[:, cols])


def _sigmoid(x):
    return 1.0 / (1.0 + jnp.exp(-x))


def _for_row_chunks(n_rows, fn):
    def body(c, carry):
        fn(pl.multiple_of(c * EPILOGUE_ROWS, EPILOGUE_ROWS))
        return carry
    lax.fori_loop(0, n_rows // EPILOGUE_ROWS, body, 0)


def _dot_into(acc_ref, terms, accumulate):
    n = acc_ref.shape[1]
    tn = _tile(n, ACC_CHUNK_COLS)
    for c in range(n // tn):
        cols = slice(c * tn, (c + 1) * tn)
        prod = None
        for a, w_ref in terms:
            term = jnp.dot(a, w_ref[:, cols], preferred_element_type=F32)
            prod = term if prod is None else prod + term
        acc_ref[:, cols] = acc_ref[:, cols] + prod if accumulate else prod


def _residual_steps(step, n_slabs, acc_ref, x_ref, scale):
    slab_rows = x_ref.shape[0]

    @pl.when(step < n_slabs)
    def _():
        slab0 = step * slab_rows

        def add(r):
            rows = pl.ds(pl.multiple_of(slab0 + r, EPILOGUE_ROWS), EPILOGUE_ROWS)
            acc_ref[rows, :] = acc_ref[rows, :] + scale * x_ref[pl.ds(r, EPILOGUE_ROWS), :]
        _for_row_chunks(slab_rows, add)


def _norm_steps(step, n_main, slab_rows, fn):
    @pl.when(step >= n_main)
    def _():
        slab0 = (step - n_main) * slab_rows
        _for_row_chunks(slab_rows, lambda r: fn(pl.ds(pl.multiple_of(slab0 + r, EPILOGUE_ROWS), EPILOGUE_ROWS),
                                                pl.ds(r, EPILOGUE_ROWS)))


def _slab_spec_factory(tm, width, n_main, n_slabs):
    rows = tm // n_slabs

    def late(cols=width):
        return pl.BlockSpec((rows, cols), lambda i, s: (i * n_slabs + jnp.clip(s - n_main, 0, n_slabs - 1), 0))

    def early(cols=width):
        return pl.BlockSpec((rows, cols), lambda i, s: (i * n_slabs + jnp.minimum(s, n_slabs - 1), 0))
    return late, early


def _ffn_kernel(xb_ref, wg0_ref, wg1_ref, wu0_ref, wu1_ref, wd0_ref, wd1_ref, x_ref, g_ref, b_ref,
                o32_ref, o16_ref, acc_ref, *, alpha, nf, n_slabs):
    s = pl.program_id(1)
    n_main = (nf + 1) // 2

    def terms(n_chunks):
        xb = xb_ref[...]
        out = []
        for wg_ref, wu_ref, wd_ref in ((wg0_ref, wu0_ref, wd0_ref), (wg1_ref, wu1_ref, wd1_ref))[:n_chunks]:
            gt = jnp.dot(xb, wg_ref[...], preferred_element_type=F32)
            up = jnp.dot(xb, wu_ref[...], preferred_element_type=F32)
            out.append((((gt * _sigmoid(gt)) * up).astype(BF16), wd_ref))
        return out

    n_full = nf // 2

    @pl.when(s == 0)
    def _():
        _dot_into(acc_ref, terms(2 if n_full > 0 else 1), accumulate=False)

    @pl.when((s > 0) & (s < n_full))
    def _():
        _dot_into(acc_ref, terms(2), accumulate=True)

    if nf % 2 and n_full > 0:
        @pl.when(s == n_full)
        def _():
            _dot_into(acc_ref, terms(1), accumulate=True)

    _residual_steps(s, n_slabs, acc_ref, x_ref, 2.0 * alpha)

    def finish(acc_rows, rows):
        def write(cols, y):
            o32_ref[rows, cols] = y
            o16_ref[rows, cols] = y.astype(BF16)
        _layer_norm_chunked(lambda cols: 0.5 * acc_ref[acc_rows, cols], acc_ref.shape[1], g_ref, b_ref, write)

    _norm_steps(s, n_main, o32_ref.shape[0], finish)


def _ffn(x, xb, wg, wu, wd, g, b, alpha):
    M, D = x.shape
    F = wg.shape[1]
    tf = _tile(F, 256)
    nf = F // tf
    n_main = (nf + 1) // 2
    n_slabs = min(FFN_NORM_SLABS, n_main)
    tm = _tile(M, 1024, EPILOGUE_ROWS * n_slabs)
    late, early = _slab_spec_factory(tm, D, n_main, n_slabs)
    last = nf - 1
    vec = pl.BlockSpec((1, D), lambda i, s: (0, 0))

    def cols(which):
        return pl.BlockSpec((D, tf), lambda i, s: (0, jnp.minimum(2 * s + which, last)))

    def rows(which):
        return pl.BlockSpec((tf, D), lambda i, s: (jnp.minimum(2 * s + which, last), 0))

    return pl.pallas_call(
        functools.partial(_ffn_kernel, alpha=alpha, nf=nf, n_slabs=n_slabs),
        grid=(M // tm, n_main + n_slabs),
        in_specs=[
            pl.BlockSpec((tm, D), lambda i, s: (i, 0), pipeline_mode=pl.Buffered(1)),
            cols(0), cols(1), cols(0), cols(1), rows(0), rows(1),
            early(), vec, vec,
        ],
        out_specs=[late(), late()],
        out_shape=[jax.ShapeDtypeStruct((M, D), F32), jax.ShapeDtypeStruct((M, D), BF16)],
        scratch_shapes=[pltpu.VMEM((tm, D), F32)],
        compiler_params=_params("parallel", "arbitrary", vmem_limit_bytes=FFN_VMEM_LIMIT_BYTES),
        name="ffn",
    )(xb, wg, wg, wu, wu, wd, wd, x, g, b)


def _glu_proj_kernel(x_ref, wa_ref, wg_ref, ba_ref, bg_ref, o_ref):
    x = x_ref[...]
    a = jnp.dot(x, wa_ref[...], preferred_element_type=F32) + ba_ref[...]
    gt = jnp.dot(x, wg_ref[...], preferred_element_type=F32) + bg_ref[...]
    o_ref[...] = a * _sigmoid(gt)


def _glu_proj(xb, w_in, b_in, d_conv):
    M, D = xb.shape
    tm = _tile(M, 1024, ROWS_BF16_TILE)
    tn = _tile(d_conv, 512)
    off = d_conv // tn
    return pl.pallas_call(
        _glu_proj_kernel,
        grid=(M // tm, d_conv // tn),
        in_specs=[
            pl.BlockSpec((tm, D), lambda i, j: (i, 0)),
            pl.BlockSpec((D, tn), lambda i, j: (0, j)),
            pl.BlockSpec((D, tn), lambda i, j: (0, j + off)),
            pl.BlockSpec((1, tn), lambda i, j: (0, j)),
            pl.BlockSpec((1, tn), lambda i, j: (0, j + off)),
        ],
        out_specs=pl.BlockSpec((tm, tn), lambda i, j: (i, j)),
        out_shape=jax.ShapeDtypeStruct((M, d_conv), F32),
        compiler_params=_params("parallel", "arbitrary"),
        name="glu_proj",
    )(xb, w_in, w_in, b_in, b_in)


def _qkv_proj_kernel(x_ref, w_ref, b_ref, o_ref, *scratch, dilation):
    res = jnp.dot(x_ref[...], w_ref[...], preferred_element_type=F32) + b_ref[...]
    if dilation == 1:
        o_ref[0, 0] = res.astype(o_ref.dtype)
    else:
        res_ref, tmp_ref = scratch
        tile_rows = res_ref.shape[1]
        d1 = tmp_ref.shape[0]
        d2 = dilation // d1
        for c in range(res_ref.shape[0]):
            lanes = slice(c * LANES, (c + 1) * LANES)
            res_ref[c] = res[:, lanes]
            for r1 in range(d1):
                if d2 == 1:
                    o_ref[0, r1, :, lanes] = res_ref[c, pl.ds(r1, tile_rows // d1, stride=d1), :].astype(o_ref.dtype)
                else:
                    tmp_ref[r1] = res_ref[c, pl.ds(r1, tile_rows // d1, stride=d1), :]
            if d2 > 1:
                for r1 in range(d1):
                    for r2 in range(d2):
                        o_ref[0, r2 * d1 + r1, :, lanes] = (
                            tmp_ref[r1, pl.ds(r2, tile_rows // dilation, stride=d2), :].astype(o_ref.dtype))


def _qkv_proj(xb, w_in, b_in, col_q, B, S, group, dilation):
    M, D = xb.shape
    L = S // dilation
    tm = _tile(S, 1024, ROWS_BF16_TILE * dilation)
    tn = _tile(math.gcd(D_GROUP, col_q), 1024)
    per_kind = D_GROUP // tn
    tiles_per_seq = S // tm
    d1 = min(dilation, MAX_CLEAN_STRIDE)
    assert dilation % d1 == 0 and dilation // d1 <= MAX_CLEAN_STRIDE, dilation

    def w_col(i, j):
        kind = j // per_kind
        return (0, (col_q + kind * D_ATTN + group * D_GROUP) // tn + j % per_kind)

    return pl.pallas_call(
        functools.partial(_qkv_proj_kernel, dilation=dilation),
        grid=(M // tm, 3 * per_kind),
        in_specs=[
            pl.BlockSpec((tm, D), lambda i, j: (i, 0)),
            pl.BlockSpec((D, tn), w_col),
            pl.BlockSpec((1, tn), w_col),
        ],
        out_specs=pl.BlockSpec((1, dilation, tm // dilation, tn),
                               lambda i, j: (i // tiles_per_seq, 0, i % tiles_per_seq, j)),
        out_shape=jax.ShapeDtypeStruct((B, dilation, L, 3 * D_GROUP), BF16),
        scratch_shapes=[] if dilation == 1 else [pltpu.VMEM((tn // LANES, tm, LANES), F32),
                                                 pltpu.VMEM((d1, tm // d1, LANES), F32)],
        compiler_params=_params("parallel", "arbitrary"),
        name=f"qkv_proj_d{dilation}",
    )(xb, w_in, b_in)


def _conv_kernel(prev_ref, cur_ref, next_ref, dw_ref, dwb_ref, g_ref, b_ref, o_ref, buf_ref, shift_ref,
                 *, ts, n_tiles):
    i = pl.program_id(1)
    buf_ref[0:CONV_HALO, :] = jnp.where(i > 0, prev_ref[0], 0.0)
    buf_ref[CONV_HALO:CONV_HALO + ts, :] = cur_ref[0]
    buf_ref[CONV_HALO + ts:, :] = jnp.where(i < n_tiles - 1, next_ref[0], 0.0)
    n_shift = shift_ref.shape[1]
    for s in range(1, SUBLANES_F32):
        shift_ref[s - 1] = buf_ref[s:s + n_shift, :]
    first = CONV_HALO - CONV_WIDTH // 2
    for c in range(ts // CONV_ROWS):
        acc = None
        for w in range(CONV_WIDTH):
            row = first + c * CONV_ROWS + w
            s, base = row % SUBLANES_F32, row - row % SUBLANES_F32
            src = buf_ref if s == 0 else shift_ref.at[s - 1]
            tap = jnp.concatenate([dw_ref[w]] * (CONV_ROWS // SUBLANES_F32), axis=0)
            term = src[base:base + CONV_ROWS, :] * tap
            acc = term if acc is None else acc + term
        y = _layer_norm(acc + dwb_ref[...], g_ref[...], b_ref[...])
        o_ref[0, c * CONV_ROWS:(c + 1) * CONV_ROWS, :] = (y * _sigmoid(y)).astype(o_ref.dtype)


def _conv_module(h, dw, dwb, g, b):
    B, S, C = h.shape
    ts = _tile(S, 256, CONV_HALO)
    n_tiles = S // ts
    hb = ts // CONV_HALO
    n_halo = S // CONV_HALO
    row = lambda bi, i: (0, 0)
    n_buf = ts + 2 * CONV_HALO
    taps = jnp.broadcast_to(dw[:, None, :], (CONV_WIDTH, SUBLANES_F32, C))
    return pl.pallas_call(
        functools.partial(_conv_kernel, ts=ts, n_tiles=n_tiles),
        grid=(B, n_tiles),
        in_specs=[
            pl.BlockSpec((1, CONV_HALO, C), lambda bi, i: (bi, jnp.maximum(i * hb - 1, 0), 0)),
            pl.BlockSpec((1, ts, C), lambda bi, i: (bi, i, 0)),
            pl.BlockSpec((1, CONV_HALO, C), lambda bi, i: (bi, jnp.minimum((i + 1) * hb, n_halo - 1), 0)),
            pl.BlockSpec((CONV_WIDTH, SUBLANES_F32, C), lambda bi, i: (0, 0, 0)),
            pl.BlockSpec((1, C), row),
            pl.BlockSpec((1, C), row),
            pl.BlockSpec((1, C), row),
        ],
        out_specs=pl.BlockSpec((1, ts, C), lambda bi, i: (bi, i, 0)),
        out_shape=jax.ShapeDtypeStruct((B, S, C), BF16),
        scratch_shapes=[pltpu.VMEM((n_buf, C), F32),
                        pltpu.VMEM((SUBLANES_F32 - 1, n_buf - SUBLANES_F32, C), F32)],
        compiler_params=_params("parallel", "arbitrary"),
        name="conv_module",
    )(h, h, h, taps, dwb, g, b)


def _t5_bucket(rel):
    half = NUM_BUCKETS // 2
    max_exact = half // 2
    ret = (rel > 0).astype(np.int32) * half
    n = np.abs(rel)
    large = max_exact + (np.log(np.maximum(n, 1) / max_exact) / np.log(MAX_DISTANCE / max_exact)
                         * (half - max_exact)).astype(np.int32)
    large = np.minimum(large, half - 1)
    return (ret + np.where(n < max_exact, n, large)).astype(np.int32)


def _band_bias(rel_bias, group, dilation):
    qi = np.arange(ATT_TQ)[:, None]
    kj = np.arange(ATT_TQ + 2 * ATT_R)[None, :]
    rel = kj - ATT_R - qi
    band = np.abs(rel) <= ATT_R
    bucket = _t5_bucket(np.clip(rel, -ATT_R, ATT_R) * dilation)
    table = rel_bias[:, group * HEADS_PER_GROUP:(group + 1) * HEADS_PER_GROUP].astype(F32)
    onehot = bucket[None, :, :, None] == np.arange(NUM_BUCKETS)
    bias = jnp.sum(jnp.where(onehot, table.T[:, None, None, :], 0.0), axis=-1)
    return jnp.where(band[None], bias, NEG_INF)


def _attn_kernel(q_ref, kp_ref, kc_ref, kn_ref, vp_ref, vc_ref, vn_ref, bias_ref, o_ref, lse_ref,
                 kbuf_ref, vbuf_ref, *, sub_len, scale):
    i = pl.program_id(2)
    n_res, tq = q_ref.shape[1], q_ref.shape[2]
    n_keys = ATT_TQ + 2 * ATT_R
    for r in range(n_res):
        for buf, p, c, n in ((kbuf_ref, kp_ref, kc_ref, kn_ref), (vbuf_ref, vp_ref, vc_ref, vn_ref)):
            buf[r, 0:ATT_R, :] = p[0, r]
            buf[r, ATT_R:ATT_R + tq, :] = c[0, r]
            buf[r, ATT_R + tq:, :] = n[0, r]
    for r in range(n_res):
        for sub in range(tq // ATT_TQ):
            q_rows = slice(sub * ATT_TQ, (sub + 1) * ATT_TQ)
            k_rows = slice(sub * ATT_TQ, sub * ATT_TQ + n_keys)
            key_pos = i * tq + sub * ATT_TQ - ATT_R + lax.broadcasted_iota(jnp.int32, (1, n_keys), 1)
            valid = (key_pos >= 0) & (key_pos < sub_len)
            lane = lax.broadcasted_iota(jnp.int32, (ATT_TQ, LANES), 1)
            lse_tile = jnp.zeros((ATT_TQ, LANES), F32)
            for h in range(HEADS_PER_GROUP):
                cols = slice(h * HEAD_DIM, (h + 1) * HEAD_DIM)
                s = lax.dot_general(q_ref[0, r, q_rows, cols], kbuf_ref[r, k_rows, cols], (((1,), (1,)), ((), ())),
                                    preferred_element_type=F32) * scale
                s = jnp.where(valid, s + bias_ref[h], NEG_INF)
                m = jnp.max(s, axis=-1, keepdims=True)
                p = jnp.exp(s - m)
                l = jnp.sum(p, axis=-1, keepdims=True)
                p = p * (1.0 / l)
                o_ref[0, r, q_rows, cols] = jnp.dot(p.astype(BF16), vbuf_ref[r, k_rows, cols],
                                                    preferred_element_type=F32)
                lse_tile = jnp.where(lane == h, m + jnp.log(l), lse_tile)
            lse_ref[0, r, q_rows, :] = lse_tile


def _attn_group(qkv, bias):
    B, dilation, L, _ = qkv.shape
    assert L % ATT_TQ == 0, (L, ATT_TQ)
    tq = _tile(L, ATT_STEP_QUERIES, ATT_TQ)
    n_res = _tile(dilation, max(ATT_STEP_QUERIES // tq, 1), 1)
    hb = tq // ATT_R
    n_halo = L // ATT_R

    def cur(kind):
        return pl.BlockSpec((1, n_res, tq, D_GROUP), lambda b, r, i: (b, r, i, kind))

    def before(kind):
        return pl.BlockSpec((1, n_res, ATT_R, D_GROUP), lambda b, r, i: (b, r, jnp.maximum(i * hb - 1, 0), kind))

    def after(kind):
        return pl.BlockSpec((1, n_res, ATT_R, D_GROUP),
                            lambda b, r, i: (b, r, jnp.minimum((i + 1) * hb, n_halo - 1), kind))

    out_spec = pl.BlockSpec((1, n_res, tq, D_GROUP), lambda b, r, i: (b, r, i, 0))
    out_sds = jax.ShapeDtypeStruct((B, dilation, L, D_GROUP), F32)
    buf = pltpu.VMEM((n_res, tq + 2 * ATT_R, D_GROUP), BF16)
    return pl.pallas_call(
        functools.partial(_attn_kernel, sub_len=L, scale=HEAD_DIM ** -0.5),
        grid=(B, dilation // n_res, L // tq),
        in_specs=[cur(0), before(1), cur(1), after(1), before(2), cur(2), after(2),
                  pl.BlockSpec((HEADS_PER_GROUP, ATT_TQ, ATT_TQ + 2 * ATT_R), lambda b, r, i: (0, 0, 0))],
        out_specs=[out_spec, pl.BlockSpec((1, n_res, tq, LANES), lambda b, r, i: (b, r, i, 0))],
        out_shape=[out_sds, jax.ShapeDtypeStruct((B, dilation, L, LANES), F32)],
        scratch_shapes=[buf, buf],
        compiler_params=_params("parallel", "parallel", "arbitrary"),
        name=f"attn_d{dilation}",
    )(qkv, qkv, qkv, qkv, qkv, qkv, qkv, bias)


def _combine_kernel(*refs):
    n = N_GROUPS
    o_refs, l_refs, out_ref, buf_ref = refs[:n], refs[n:2 * n], refs[2 * n], refs[2 * n + 1]

    def token_order(ref, slot, lanes):
        dilation, rows = ref.shape[1], ref.shape[2]
        if dilation == 1:
            return ref[0, 0, :, lanes]
        for r in range(dilation):
            buf_ref[slot, pl.ds(r, rows, stride=dilation), :] = ref[0, r, :, lanes]
        return buf_ref[slot]

    lses = [token_order(ref, n + g, slice(0, LANES)) for g, ref in enumerate(l_refs)]
    mx = functools.reduce(jnp.maximum, lses)
    es = [jnp.exp(l - mx) for l in lses]
    den = functools.reduce(lambda a, b: a + b, es)
    weights = [e / den for e in es]
    for h in range(out_ref.shape[2] // HEAD_DIM):
        cols = slice(h * HEAD_DIM, (h + 1) * HEAD_DIM)
        acc = None
        for g, ref in enumerate(o_refs):
            term = weights[g][:, h:h + 1] * token_order(ref, g, cols)
            acc = term if acc is None else acc + term
        out_ref[0, :, cols] = acc.astype(out_ref.dtype)


def _combine(outs, lses, S):
    B = outs[0].shape[0]
    N = outs[0].shape[-1]
    assert HEAD_DIM == LANES
    max_dil = max(o.shape[1] for o in outs)
    tm = _tile(S, 512, SUBLANES_F32 * max_dil)

    def spec(a):
        d = a.shape[1]
        return pl.BlockSpec((1, d, tm // d, a.shape[-1]), lambda b, i: (b, 0, i, 0))

    return pl.pallas_call(
        _combine_kernel,
        grid=(B, S // tm),
        in_specs=[spec(a) for a in (*outs, *lses)],
        out_specs=pl.BlockSpec((1, tm, N), lambda b, i: (b, i, 0)),
        out_shape=jax.ShapeDtypeStruct((B, S, N), BF16),
        scratch_shapes=[pltpu.VMEM((2 * N_GROUPS, tm, LANES), F32)],
        compiler_params=_params("parallel", "arbitrary"),
        name="attn_combine",
    )(*outs, *lses)


def _merge_kernel(x_ref, hc_ref, at_ref, wg0_ref, wg1_ref, wc_ref, wa_ref, b0_ref, b1_ref, o_ref):
    x = x_ref[...]
    g0 = _sigmoid(jnp.dot(x, wg0_ref[...], preferred_element_type=F32) + b0_ref[...])
    g1 = _sigmoid(jnp.dot(x, wg1_ref[...], preferred_element_type=F32) + b1_ref[...])
    conv_out = jnp.dot(hc_ref[...], wc_ref[...], preferred_element_type=F32)
    attn_out = jnp.dot(at_ref[...], wa_ref[...], preferred_element_type=F32)
    o_ref[...] = (g0 * conv_out + g1 * attn_out).astype(o_ref.dtype)


def _merge(xb, hc, attn, w_in, b_in, w_conv_out, w_attn_out, col_gate):
    M, D = xb.shape
    tm = _tile(M, 1024, ROWS_BF16_TILE)
    tn = _tile(math.gcd(D, col_gate), 256)
    off0 = col_gate // tn
    off1 = (col_gate + D) // tn
    return pl.pallas_call(
        _merge_kernel,
        grid=(M // tm, D // tn),
        in_specs=[
            pl.BlockSpec((tm, D), lambda i, j: (i, 0)),
            pl.BlockSpec((tm, hc.shape[1]), lambda i, j: (i, 0)),
            pl.BlockSpec((tm, attn.shape[1]), lambda i, j: (i, 0)),
            pl.BlockSpec((D, tn), lambda i, j: (0, j + off0)),
            pl.BlockSpec((D, tn), lambda i, j: (0, j + off1)),
            pl.BlockSpec((hc.shape[1], tn), lambda i, j: (0, j)),
            pl.BlockSpec((attn.shape[1], tn), lambda i, j: (0, j)),
            pl.BlockSpec((1, tn), lambda i, j: (0, j + off0)),
            pl.BlockSpec((1, tn), lambda i, j: (0, j + off1)),
        ],
        out_specs=pl.BlockSpec((tm, tn), lambda i, j: (i, j)),
        out_shape=jax.ShapeDtypeStruct((M, D), BF16),
        compiler_params=_params("parallel", "arbitrary"),
        name="merge",
    )(xb, hc, attn, w_in, w_in, w_conv_out, w_attn_out, b_in, b_in)


def _zero_rows(ref, n_rows):
    def zero(r):
        ref[pl.ds(r, EPILOGUE_ROWS), :] = jnp.zeros((EPILOGUE_ROWS, ref.shape[1]), ref.dtype)
    _for_row_chunks(n_rows, zero)


def _deferred_norm_steps(n_tiles, acc_refs, main, norm):
    i = pl.program_id(0)
    k = pl.program_id(1)

    @pl.when((i == 0) & (k == 0))
    def _():
        for acc_ref in acc_refs:
            _zero_rows(acc_ref, acc_ref.shape[0])

    @pl.when(i == 0)
    def _():
        main(acc_refs[0])

    for parity in (0, 1):
        @pl.when((i > 0) & (i < n_tiles) & (i % 2 == parity))
        def _():
            norm(acc_refs[1 - parity])
            main(acc_refs[parity])

    @pl.when(i == n_tiles)
    def _():
        norm(acc_refs[(n_tiles - 1) % 2])


def _slab_chunks(k, slab_rows):
    for c in range(slab_rows // EPILOGUE_ROWS):
        r = c * EPILOGUE_ROWS
        yield (pl.ds(pl.multiple_of(k * slab_rows + r, EPILOGUE_ROWS), EPILOGUE_ROWS), pl.ds(r, EPILOGUE_ROWS))


def _deferred_specs(n_tiles, n_steps, tm, D):
    rows = tm // n_steps

    def main_step(i, k):
        return jnp.where(i < n_tiles, k, n_steps - 1)

    def main_tile(i):
        return jnp.minimum(i, n_tiles - 1)

    def a_spec(tk):
        return pl.BlockSpec((tm, tk), lambda i, k: (main_tile(i), main_step(i, k)))

    def w_spec(tk):
        return pl.BlockSpec((tk, D), lambda i, k: (main_step(i, k), 0))

    def main_slab(cols=D):
        return pl.BlockSpec((rows, cols), lambda i, k: (main_tile(i) * n_steps + main_step(i, k), 0))

    def norm_slab(cols=D):
        return pl.BlockSpec((rows, cols), lambda i, k: (jnp.maximum(i - 1, 0) * n_steps + jnp.where(i > 0, k, 0), 0))

    return a_spec, w_spec, main_slab, norm_slab


def _out_proj_kernel(a_ref, w_ref, x_ref, g_ref, b_ref, o32_ref, o16_ref, acc0_ref, acc1_ref, *, alpha, n_tiles):
    k = pl.program_id(1)
    slab_rows = x_ref.shape[0]

    def main(acc_ref):
        _dot_into(acc_ref, [(a_ref[...], w_ref)], accumulate=True)
        for acc_rows, rows in _slab_chunks(k, slab_rows):
            acc_ref[acc_rows, :] = acc_ref[acc_rows, :] + alpha * x_ref[rows, :]

    def norm(acc_ref):
        for acc_rows, rows in _slab_chunks(k, slab_rows):
            y = _layer_norm(acc_ref[acc_rows, :], g_ref[...], b_ref[...])
            o32_ref[rows, :] = y
            o16_ref[rows, :] = y.astype(BF16)
            acc_ref[acc_rows, :] = jnp.zeros_like(y)

    _deferred_norm_steps(n_tiles, (acc0_ref, acc1_ref), main, norm)


def _out_proj(a, w, x, g, b, alpha):
    M, K = a.shape
    D = w.shape[1]
    tk = _tile(K, 1024)
    nk = K // tk
    tm = _tile(M, 512, EPILOGUE_ROWS * nk)
    n_tiles = M // tm
    a_spec, w_spec, main_slab, norm_slab = _deferred_specs(n_tiles, nk, tm, D)
    vec = pl.BlockSpec((1, D), lambda i, k: (0, 0))
    return pl.pallas_call(
        functools.partial(_out_proj_kernel, alpha=alpha, n_tiles=n_tiles),
        grid=(n_tiles + 1, nk),
        in_specs=[a_spec(tk), w_spec(tk), main_slab(), vec, vec],
        out_specs=[norm_slab(), norm_slab()],
        out_shape=[jax.ShapeDtypeStruct((M, D), F32), jax.ShapeDtypeStruct((M, D), BF16)],
        scratch_shapes=[pltpu.VMEM((tm, D), F32), pltpu.VMEM((tm, D), F32)],
        compiler_params=_params("arbitrary", "arbitrary"),
        name="out_proj",
    )(a, w, x, g, b)


def _ple_kernel(a_ref, w_ref, x_ref, p_ref, wp_ref, bg_ref, g_ref, b_ref, o_ref, acc0_ref, acc1_ref,
                *, alpha, n_tiles):
    k = pl.program_id(1)
    slab_rows = x_ref.shape[0]

    def main(acc_ref):
        _dot_into(acc_ref, [(a_ref[...], w_ref)], accumulate=True)

    def norm(acc_ref):
        for acc_rows, rows in _slab_chunks(k, slab_rows):
            gate = _sigmoid(acc_ref[acc_rows, :] + bg_ref[...])
            pw = jnp.dot(p_ref[rows, :].astype(BF16), wp_ref[...], preferred_element_type=F32)
            o_ref[rows, :] = _layer_norm(alpha * x_ref[rows, :] + pw * gate, g_ref[...], b_ref[...])
            acc_ref[acc_rows, :] = jnp.zeros_like(gate)

    _deferred_norm_steps(n_tiles, (acc0_ref, acc1_ref), main, norm)


def _ple(a, w, x, p, wp, bg, g, b, alpha):
    M, K = a.shape
    D = w.shape[1]
    P = p.shape[1]
    tk = _tile(K, 1024)
    nk = K // tk
    tm = _tile(M, 512, EPILOGUE_ROWS * nk)
    n_tiles = M // tm
    a_spec, w_spec, _, norm_slab = _deferred_specs(n_tiles, nk, tm, D)
    vec = pl.BlockSpec((1, D), lambda i, k: (0, 0))
    return pl.pallas_call(
        functools.partial(_ple_kernel, alpha=alpha, n_tiles=n_tiles),
        grid=(n_tiles + 1, nk),
        in_specs=[a_spec(tk), w_spec(tk), norm_slab(), norm_slab(P),
                  pl.BlockSpec((P, D), lambda i, k: (0, 0)), vec, vec, vec],
        out_specs=norm_slab(),
        out_shape=jax.ShapeDtypeStruct((M, D), F32),
        scratch_shapes=[pltpu.VMEM((tm, D), F32), pltpu.VMEM((tm, D), F32)],
        compiler_params=_params("arbitrary", "arbitrary"),
        name="ple",
    )(a, w, x, p, wp, bg, g, b)


def _encoder_layer(x, p, B, S, w, biases, alpha):
    d_conv = w["conv_dw"].shape[1]
    col_q = 2 * d_conv
    col_gate = col_q + 3 * D_ATTN
    ln_g, ln_b = w["ln_g"], w["ln_b"]

    x1, x1b = _ffn(x, x.astype(BF16), w["w_ff1_gate"], w["w_ff1_up"], w["w_ff1_down"], ln_g[0:1], ln_b[0:1], alpha)

    h = _glu_proj(x1b, w["w_in"], w["b_in"], d_conv)
    hc = _conv_module(h.reshape(B, S, d_conv), w["conv_dw"], w["conv_dw_b"], w["conv_ln_g"], w["conv_ln_b"])
    hc = hc.reshape(B * S, d_conv)

    outs, lses = [], []
    for gi, (_, dil) in enumerate(DILATED_GROUPS):
        qkv = _qkv_proj(x1b, w["w_in"], w["b_in"], col_q, B, S, gi, dil)
        o, lse = _attn_group(qkv, biases[gi])
        outs.append(o)
        lses.append(lse)
    attn = _combine(outs, lses, S).reshape(B * S, D_GROUP)

    merged = _merge(x1b, hc, attn, w["w_in"], w["b_in"], w["w_conv_out"], w["w_attn_out"], col_gate)
    x2, x2b = _out_proj(merged, w["w_out"], x1, ln_g[1:2], ln_b[1:2], alpha)
    x3, x3b = _ffn(x2, x2b, w["w_ff2_gate"], w["w_ff2_up"], w["w_ff2_down"], ln_g[2:3], ln_b[2:3], alpha)
    return _ple(x3b, w["w_ple_gate"], x3, p, w["w_ple"], w["b_ple_gate"], ln_g[3:4], ln_b[3:4], alpha)


_MATMUL_WEIGHTS = ("w_ff1_gate", "w_ff1_up", "w_ff1_down", "w_in", "w_conv_out", "w_attn_out", "w_out",
                   "w_ff2_gate", "w_ff2_up", "w_ff2_down", "w_ple", "w_ple_gate")
_ROW_VECTORS = ("b_in", "conv_dw_b", "conv_ln_g", "conv_ln_b", "b_ple_gate")


def kernel(x_prompt, x_sample, p_prompt, p_sample, rel_bias, ln_g, ln_b, w_ff1_gate, w_ff1_up, w_ff1_down,
           w_in, b_in, conv_dw, conv_dw_b, conv_ln_g, conv_ln_b, w_conv_out, w_attn_out, w_out, w_ff2_gate,
           w_ff2_up, w_ff2_down, w_ple, w_ple_gate, b_ple_gate):
    stacked = dict(ln_g=ln_g, ln_b=ln_b, w_ff1_gate=w_ff1_gate, w_ff1_up=w_ff1_up, w_ff1_down=w_ff1_down,
                   w_in=w_in, b_in=b_in, conv_dw=conv_dw, conv_dw_b=conv_dw_b, conv_ln_g=conv_ln_g,
                   conv_ln_b=conv_ln_b, w_conv_out=w_conv_out, w_attn_out=w_attn_out, w_out=w_out,
                   w_ff2_gate=w_ff2_gate, w_ff2_up=w_ff2_up, w_ff2_down=w_ff2_down, w_ple=w_ple,
                   w_ple_gate=w_ple_gate, b_ple_gate=b_ple_gate)
    depth = ln_g.shape[0]
    alpha = (2.0 * depth) ** 0.25
    layers = []
    for i in range(depth):
        w = {k: v[i] for k, v in stacked.items()}
        for k in _MATMUL_WEIGHTS:
            w[k] = w[k].astype(BF16)
        for k in _ROW_VECTORS:
            w[k] = w[k].reshape(1, -1)
        layers.append(w)
    biases = [_band_bias(rel_bias, gi, dil) for gi, (_, dil) in enumerate(DILATED_GROUPS)]

    def run(x, p):
        B, S, D = x.shape
        y = x.reshape(B * S, D)
        for i, w in enumerate(layers):
            y = _encoder_layer(y, p[i].reshape(B * S, -1), B, S, w, biases, alpha)
        return y.reshape(B, S, D)

    return run(x_prompt, p_prompt), run(x_sample, p_sample)
```

```python
import functools
import math

import jax
import jax.numpy as jnp
import numpy as np
from jax import lax
from jax.experimental import pallas as pl
from jax.experimental.pallas import tpu as pltpu

F32 = jnp.float32
BF16 = jnp.bfloat16

HEAD_DIM = 128
HEADS_PER_GROUP = 8
DILATED_GROUPS = ((128, 1), (512, 4), (2048, 16))
N_GROUPS = len(DILATED_GROUPS)
D_GROUP = HEADS_PER_GROUP * HEAD_DIM
D_ATTN = N_GROUPS * D_GROUP
CONV_WIDTH = 31
NUM_BUCKETS = 32
MAX_DISTANCE = 1024
LN_EPS = 1e-5
NEG_INF = -1e30

LANES = 128
SUBLANES_F32 = 8
ROWS_BF16_TILE = 16
MAX_CLEAN_STRIDE = 4
VMEM_LIMIT_BYTES = 58 * 1024 * 1024
FFN_VMEM_LIMIT_BYTES = 63 * 1024 * 1024

ATT_R = 64
ATT_TQ = 128
ATT_STEP_QUERIES = 1024
CONV_HALO = 16
CONV_ROWS = 16
ACC_CHUNK_COLS = 512
EPILOGUE_ROWS = 64
LN_CHUNK_COLS = 512
FFN_NORM_SLABS = 16


def _tile(n, target, align=LANES):
    if n <= target:
        return n
    t = (target // align) * align
    while t >= align:
        if n % t == 0:
            return t
        t -= align
    raise ValueError(f"no tile for {n} <= {target} aligned to {align}")


def _params(*sem, vmem_limit_bytes=VMEM_LIMIT_BYTES):
    return pltpu.CompilerParams(dimension_semantics=sem, vmem_limit_bytes=vmem_limit_bytes)


def _layer_norm(r, g, b):
    mu = jnp.mean(r, axis=-1, keepdims=True)
    c = r - mu
    var = jnp.mean(c * c, axis=-1, keepdims=True)
    return c * lax.rsqrt(var + LN_EPS) * g + b


def _layer_norm_chunked(read, n_cols, g_ref, b_ref, write):
    chunks = [slice(c, c + LN_CHUNK_COLS) for c in range(0, n_cols, LN_CHUNK_COLS)]
    total = functools.reduce(lambda a, b: a + b, [read(cols) for cols in chunks])
    mu = jnp.sum(total, axis=-1, keepdims=True) / n_cols
    sq = None
    for cols in chunks:
        d = read(cols) - mu
        sq = d * d if sq is None else sq + d * d
    rstd = lax.rsqrt(jnp.sum(sq, axis=-1, keepdims=True) / n_cols + LN_EPS)
    for cols in chunks:
        write(cols, (read(cols) - mu) * rstd * g_ref[:, cols] + b_ref[:, cols])


def _sigmoid(x):
    return 1.0 / (1.0 + jnp.exp(-x))


def _for_row_chunks(n_rows, fn):
    def body(c, carry):
        fn(pl.multiple_of(c * EPILOGUE_ROWS, EPILOGUE_ROWS))
        return carry
    lax.fori_loop(0, n_rows // EPILOGUE_ROWS, body, 0)


def _dot_into(acc_ref, terms, accumulate):
    n = acc_ref.shape[1]
    tn = _tile(n, ACC_CHUNK_COLS)
    for c in range(n // tn):
        cols = slice(c * tn, (c + 1) * tn)
        prod = None
        for a, w_ref in terms:
            term = jnp.dot(a, w_ref[:, cols], preferred_element_type=F32)
            prod = term if prod is None else prod + term
        acc_ref[:, cols] = acc_ref[:, cols] + prod if accumulate else prod


def _norm_steps(step, n_main, slab_rows, fn):
    @pl.when(step >= n_main)
    def _():
        slab0 = (step - n_main) * slab_rows
        _for_row_chunks(slab_rows, lambda r: fn(pl.ds(pl.multiple_of(slab0 + r, EPILOGUE_ROWS), EPILOGUE_ROWS),
                                                pl.ds(r, EPILOGUE_ROWS)))


def _slab_spec_factory(tm, width, n_main, n_slabs):
    rows = tm // n_slabs

    def late(cols=width):
        return pl.BlockSpec((rows, cols), lambda i, s: (i * n_slabs + jnp.clip(s - n_main, 0, n_slabs - 1), 0))

    def early(cols=width):
        return pl.BlockSpec((rows, cols), lambda i, s: (i * n_slabs + jnp.minimum(s, n_slabs - 1), 0))
    return late, early


def _ffn_kernel(xb_ref, wg0_ref, wg1_ref, wu0_ref, wu1_ref, wd0_ref, wd1_ref, x_ref, g_ref, b_ref,
                o32_ref, o16_ref, acc_ref, *, alpha, nf, n_slabs):
    s = pl.program_id(1)
    n_main = (nf + 1) // 2

    def terms(n_chunks):
        xb = xb_ref[...]
        out = []
        for wg_ref, wu_ref, wd_ref in ((wg0_ref, wu0_ref, wd0_ref), (wg1_ref, wu1_ref, wd1_ref))[:n_chunks]:
            gt = jnp.dot(xb, wg_ref[...], preferred_element_type=F32)
            up = jnp.dot(xb, wu_ref[...], preferred_element_type=F32)
            out.append((((gt * _sigmoid(gt)) * up).astype(BF16), wd_ref))
        return out

    n_full = nf // 2

    slab_rows = x_ref.shape[0]

    def main(n_chunks, accumulate, with_residual):
        _dot_into(acc_ref, terms(n_chunks), accumulate)
        if with_residual:
            for acc_rows, rows in _slab_chunks(s, slab_rows):
                acc_ref[acc_rows, :] = acc_ref[acc_rows, :] + (2.0 * alpha) * x_ref[rows, :]

    first_chunks = 2 if n_full > 0 else 1
    pl.when(s == 0)(functools.partial(main, first_chunks, False, True))
    pl.when((s > 0) & (s < min(n_slabs, n_full)))(functools.partial(main, 2, True, True))
    if n_slabs < n_full:
        pl.when((s >= n_slabs) & (s < n_full))(functools.partial(main, 2, True, False))
    if nf % 2 and n_full > 0:
        pl.when(s == n_full)(functools.partial(main, 1, True, n_full < n_slabs))

    def finish(acc_rows, rows):
        def write(cols, y):
            o32_ref[rows, cols] = y
            o16_ref[rows, cols] = y.astype(BF16)
        _layer_norm_chunked(lambda cols: 0.5 * acc_ref[acc_rows, cols], acc_ref.shape[1], g_ref, b_ref, write)

    _norm_steps(s, n_main, o32_ref.shape[0], finish)


def _ffn(x, xb, wg, wu, wd, g, b, alpha):
    M, D = x.shape
    F = wg.shape[1]
    tf = _tile(F, 256)
    nf = F // tf
    n_main = (nf + 1) // 2
    n_slabs = min(FFN_NORM_SLABS, n_main)
    tm = _tile(M, 1024, EPILOGUE_ROWS * n_slabs)
    late, early = _slab_spec_factory(tm, D, n_main, n_slabs)
    last = nf - 1
    vec = pl.BlockSpec((1, D), lambda i, s: (0, 0))

    def cols(which):
        return pl.BlockSpec((D, tf), lambda i, s: (0, jnp.minimum(2 * s + which, last)))

    def rows(which):
        return pl.BlockSpec((tf, D), lambda i, s: (jnp.minimum(2 * s + which, last), 0))

    return pl.pallas_call(
        functools.partial(_ffn_kernel, alpha=alpha, nf=nf, n_slabs=n_slabs),
        grid=(M // tm, n_main + n_slabs),
        in_specs=[
            pl.BlockSpec((tm, D), lambda i, s: (i, 0), pipeline_mode=pl.Buffered(1)),
            cols(0), cols(1), cols(0), cols(1), rows(0), rows(1),
            early(), vec, vec,
        ],
        out_specs=[late(), late()],
        out_shape=[jax.ShapeDtypeStruct((M, D), F32), jax.ShapeDtypeStruct((M, D), BF16)],
        scratch_shapes=[pltpu.VMEM((tm, D), F32)],
        compiler_params=_params("parallel", "arbitrary", vmem_limit_bytes=FFN_VMEM_LIMIT_BYTES),
        name="ffn",
    )(xb, wg, wg, wu, wu, wd, wd, x, g, b)


def _glu_proj_kernel(x_ref, wa_ref, wg_ref, ba_ref, bg_ref, o_ref):
    x = x_ref[...]
    a = jnp.dot(x, wa_ref[...], preferred_element_type=F32) + ba_ref[...]
    gt = jnp.dot(x, wg_ref[...], preferred_element_type=F32) + bg_ref[...]
    o_ref[...] = a * _sigmoid(gt)


def _glu_proj(xb, w_in, b_in, d_conv):
    M, D = xb.shape
    tm = _tile(M, 1024, ROWS_BF16_TILE)
    tn = _tile(d_conv, 512)
    off = d_conv // tn
    return pl.pallas_call(
        _glu_proj_kernel,
        grid=(M // tm, d_conv // tn),
        in_specs=[
            pl.BlockSpec((tm, D), lambda i, j: (i, 0)),
            pl.BlockSpec((D, tn), lambda i, j: (0, j)),
            pl.BlockSpec((D, tn), lambda i, j: (0, j + off)),
            pl.BlockSpec((1, tn), lambda i, j: (0, j)),
            pl.BlockSpec((1, tn), lambda i, j: (0, j + off)),
        ],
        out_specs=pl.BlockSpec((tm, tn), lambda i, j: (i, j)),
        out_shape=jax.ShapeDtypeStruct((M, d_conv), F32),
        compiler_params=_params("parallel", "arbitrary"),
        name="glu_proj",
    )(xb, w_in, w_in, b_in, b_in)


def _qkv_proj_kernel(x_ref, w_ref, b_ref, o_ref, *scratch, dilation):
    res = jnp.dot(x_ref[...], w_ref[...], preferred_element_type=F32) + b_ref[...]
    if dilation == 1:
        o_ref[0, 0] = res.astype(o_ref.dtype)
    else:
        res_ref, tmp_ref = scratch
        tile_rows = res_ref.shape[1]
        d1 = tmp_ref.shape[0]
        d2 = dilation // d1
        for c in range(res_ref.shape[0]):
            lanes = slice(c * LANES, (c + 1) * LANES)
            res_ref[c] = res[:, lanes]
            for r1 in range(d1):
                if d2 == 1:
                    o_ref[0, r1, :, lanes] = res_ref[c, pl.ds(r1, tile_rows // d1, stride=d1), :].astype(o_ref.dtype)
                else:
                    tmp_ref[r1] = res_ref[c, pl.ds(r1, tile_rows // d1, stride=d1), :]
            if d2 > 1:
                for r1 in range(d1):
                    for r2 in range(d2):
                        o_ref[0, r2 * d1 + r1, :, lanes] = (
                            tmp_ref[r1, pl.ds(r2, tile_rows // dilation, stride=d2), :].astype(o_ref.dtype))


def _qkv_proj(xb, w_in, b_in, col_q, B, S, group, dilation):
    M, D = xb.shape
    L = S // dilation
    tm = _tile(S, 1024, ROWS_BF16_TILE * dilation)
    tn = _tile(math.gcd(D_GROUP, col_q), 1024)
    per_kind = D_GROUP // tn
    tiles_per_seq = S // tm
    d1 = min(dilation, MAX_CLEAN_STRIDE)
    assert dilation % d1 == 0 and dilation // d1 <= MAX_CLEAN_STRIDE, dilation

    def w_col(i, j):
        kind = j // per_kind
        return (0, (col_q + kind * D_ATTN + group * D_GROUP) // tn + j % per_kind)

    return pl.pallas_call(
        functools.partial(_qkv_proj_kernel, dilation=dilation),
        grid=(M // tm, 3 * per_kind),
        in_specs=[
            pl.BlockSpec((tm, D), lambda i, j: (i, 0)),
            pl.BlockSpec((D, tn), w_col),
            pl.BlockSpec((1, tn), w_col),
        ],
        out_specs=pl.BlockSpec((1, dilation, tm // dilation, tn),
                               lambda i, j: (i // tiles_per_seq, 0, i % tiles_per_seq, j)),
        out_shape=jax.ShapeDtypeStruct((B, dilation, L, 3 * D_GROUP), BF16),
        scratch_shapes=[] if dilation == 1 else [pltpu.VMEM((tn // LANES, tm, LANES), F32),
                                                 pltpu.VMEM((d1, tm // d1, LANES), F32)],
        compiler_params=_params("parallel", "arbitrary"),
        name=f"qkv_proj_d{dilation}",
    )(xb, w_in, b_in)


def _conv_kernel(prev_ref, cur_ref, next_ref, dw_ref, dwb_ref, g_ref, b_ref, o_ref, buf_ref, shift_ref,
                 *, ts, n_tiles):
    i = pl.program_id(1)
    buf_ref[0:CONV_HALO, :] = jnp.where(i > 0, prev_ref[0], 0.0)
    buf_ref[CONV_HALO:CONV_HALO + ts, :] = cur_ref[0]
    buf_ref[CONV_HALO + ts:, :] = jnp.where(i < n_tiles - 1, next_ref[0], 0.0)
    n_shift = shift_ref.shape[1]
    for s in range(1, SUBLANES_F32):
        shift_ref[s - 1] = buf_ref[s:s + n_shift, :]
    first = CONV_HALO - CONV_WIDTH // 2
    for c in range(ts // CONV_ROWS):
        acc = None
        for w in range(CONV_WIDTH):
            row = first + c * CONV_ROWS + w
            s, base = row % SUBLANES_F32, row - row % SUBLANES_F32
            src = buf_ref if s == 0 else shift_ref.at[s - 1]
            tap = jnp.concatenate([dw_ref[w]] * (CONV_ROWS // SUBLANES_F32), axis=0)
            term = src[base:base + CONV_ROWS, :] * tap
            acc = term if acc is None else acc + term
        y = _layer_norm(acc + dwb_ref[...], g_ref[...], b_ref[...])
        o_ref[0, c * CONV_ROWS:(c + 1) * CONV_ROWS, :] = (y * _sigmoid(y)).astype(o_ref.dtype)


def _conv_module(h, dw, dwb, g, b):
    B, S, C = h.shape
    ts = _tile(S, 256, CONV_HALO)
    n_tiles = S // ts
    hb = ts // CONV_HALO
    n_halo = S // CONV_HALO
    row = lambda bi, i: (0, 0)
    n_buf = ts + 2 * CONV_HALO
    taps = jnp.broadcast_to(dw[:, None, :], (CONV_WIDTH, SUBLANES_F32, C))
    return pl.pallas_call(
        functools.partial(_conv_kernel, ts=ts, n_tiles=n_tiles),
        grid=(B, n_tiles),
        in_specs=[
            pl.BlockSpec((1, CONV_HALO, C), lambda bi, i: (bi, jnp.maximum(i * hb - 1, 0), 0)),
            pl.BlockSpec((1, ts, C), lambda bi, i: (bi, i, 0)),
            pl.BlockSpec((1, CONV_HALO, C), lambda bi, i: (bi, jnp.minimum((i + 1) * hb, n_halo - 1), 0)),
            pl.BlockSpec((CONV_WIDTH, SUBLANES_F32, C), lambda bi, i: (0, 0, 0)),
            pl.BlockSpec((1, C), row),
            pl.BlockSpec((1, C), row),
            pl.BlockSpec((1, C), row),
        ],
        out_specs=pl.BlockSpec((1, ts, C), lambda bi, i: (bi, i, 0)),
        out_shape=jax.ShapeDtypeStruct((B, S, C), BF16),
        scratch_shapes=[pltpu.VMEM((n_buf, C), F32),
                        pltpu.VMEM((SUBLANES_F32 - 1, n_buf - SUBLANES_F32, C), F32)],
        compiler_params=_params("parallel", "arbitrary"),
        name="conv_module",
    )(h, h, h, taps, dwb, g, b)


def _t5_bucket(rel):
    half = NUM_BUCKETS // 2
    max_exact = half // 2
    ret = (rel > 0).astype(np.int32) * half
    n = np.abs(rel)
    large = max_exact + (np.log(np.maximum(n, 1) / max_exact) / np.log(MAX_DISTANCE / max_exact)
                         * (half - max_exact)).astype(np.int32)
    large = np.minimum(large, half - 1)
    return (ret + np.where(n < max_exact, n, large)).astype(np.int32)


def _band_bias(rel_bias, group, dilation):
    qi = np.arange(ATT_TQ)[:, None]
    kj = np.arange(ATT_TQ + 2 * ATT_R)[None, :]
    rel = kj - ATT_R - qi
    band = np.abs(rel) <= ATT_R
    bucket = _t5_bucket(np.clip(rel, -ATT_R, ATT_R) * dilation)
    table = rel_bias[:, group * HEADS_PER_GROUP:(group + 1) * HEADS_PER_GROUP].astype(F32)
    onehot = bucket[None, :, :, None] == np.arange(NUM_BUCKETS)
    bias = jnp.sum(jnp.where(onehot, table.T[:, None, None, :], 0.0), axis=-1)
    return jnp.where(band[None], bias, NEG_INF)


def _attn_kernel(q_ref, kp_ref, kc_ref, kn_ref, vp_ref, vc_ref, vn_ref, bias_ref, o_ref, lse_ref,
                 kbuf_ref, vbuf_ref, *, sub_len, scale):
    i = pl.program_id(2)
    n_res, tq = q_ref.shape[1], q_ref.shape[2]
    n_keys = ATT_TQ + 2 * ATT_R
    for r in range(n_res):
        for buf, p, c, n in ((kbuf_ref, kp_ref, kc_ref, kn_ref), (vbuf_ref, vp_ref, vc_ref, vn_ref)):
            buf[r, 0:ATT_R, :] = p[0, r]
            buf[r, ATT_R:ATT_R + tq, :] = c[0, r]
            buf[r, ATT_R + tq:, :] = n[0, r]
    for r in range(n_res):
        for sub in range(tq // ATT_TQ):
            q_rows = slice(sub * ATT_TQ, (sub + 1) * ATT_TQ)
            k_rows = slice(sub * ATT_TQ, sub * ATT_TQ + n_keys)
            key_pos = i * tq + sub * ATT_TQ - ATT_R + lax.broadcasted_iota(jnp.int32, (1, n_keys), 1)
            valid = (key_pos >= 0) & (key_pos < sub_len)
            lane = lax.broadcasted_iota(jnp.int32, (ATT_TQ, LANES), 1)
            lse_tile = jnp.zeros((ATT_TQ, LANES), F32)
            for h in range(HEADS_PER_GROUP):
                cols = slice(h * HEAD_DIM, (h + 1) * HEAD_DIM)
                s = lax.dot_general(q_ref[0, r, q_rows, cols], kbuf_ref[r, k_rows, cols], (((1,), (1,)), ((), ())),
                                    preferred_element_type=F32) * scale
                s = jnp.where(valid, s + bias_ref[h], NEG_INF)
                m = jnp.max(s, axis=-1, keepdims=True)
                p = jnp.exp(s - m)
                l = jnp.sum(p, axis=-1, keepdims=True)
                p = p * (1.0 / l)
                o_ref[0, r, q_rows, cols] = jnp.dot(p.astype(BF16), vbuf_ref[r, k_rows, cols],
                                                    preferred_element_type=F32)
                lse_tile = jnp.where(lane == h, m + jnp.log(l), lse_tile)
            lse_ref[0, r, q_rows, :] = lse_tile


def _attn_group(qkv, bias):
    B, dilation, L, _ = qkv.shape
    assert L % ATT_TQ == 0, (L, ATT_TQ)
    tq = _tile(L, ATT_STEP_QUERIES, ATT_TQ)
    n_res = _tile(dilation, max(ATT_STEP_QUERIES // tq, 1), 1)
    hb = tq // ATT_R
    n_halo = L // ATT_R

    def cur(kind):
        return pl.BlockSpec((1, n_res, tq, D_GROUP), lambda b, r, i: (b, r, i, kind))

    def before(kind):
        return pl.BlockSpec((1, n_res, ATT_R, D_GROUP), lambda b, r, i: (b, r, jnp.maximum(i * hb - 1, 0), kind))

    def after(kind):
        return pl.BlockSpec((1, n_res, ATT_R, D_GROUP),
                            lambda b, r, i: (b, r, jnp.minimum((i + 1) * hb, n_halo - 1), kind))

    out_spec = pl.BlockSpec((1, n_res, tq, D_GROUP), lambda b, r, i: (b, r, i, 0))
    out_sds = jax.ShapeDtypeStruct((B, dilation, L, D_GROUP), F32)
    buf = pltpu.VMEM((n_res, tq + 2 * ATT_R, D_GROUP), BF16)
    return pl.pallas_call(
        functools.partial(_attn_kernel, sub_len=L, scale=HEAD_DIM ** -0.5),
        grid=(B, dilation // n_res, L // tq),
        in_specs=[cur(0), before(1), cur(1), after(1), before(2), cur(2), after(2),
                  pl.BlockSpec((HEADS_PER_GROUP, ATT_TQ, ATT_TQ + 2 * ATT_R), lambda b, r, i: (0, 0, 0))],
        out_specs=[out_spec, pl.BlockSpec((1, n_res, tq, LANES), lambda b, r, i: (b, r, i, 0))],
        out_shape=[out_sds, jax.ShapeDtypeStruct((B, dilation, L, LANES), F32)],
        scratch_shapes=[buf, buf],
        compiler_params=_params("parallel", "parallel", "arbitrary"),
        name=f"attn_d{dilation}",
    )(qkv, qkv, qkv, qkv, qkv, qkv, qkv, bias)


def _combine_kernel(*refs):
    n = N_GROUPS
    o_refs, l_refs, out_ref, buf_ref = refs[:n], refs[n:2 * n], refs[2 * n], refs[2 * n + 1]

    def token_order(ref, slot, lanes):
        dilation, rows = ref.shape[1], ref.shape[2]
        if dilation == 1:
            return ref[0, 0, :, lanes]
        for r in range(dilation):
            buf_ref[slot, pl.ds(r, rows, stride=dilation), :] = ref[0, r, :, lanes]
        return buf_ref[slot]

    lses = [token_order(ref, n + g, slice(0, LANES)) for g, ref in enumerate(l_refs)]
    mx = functools.reduce(jnp.maximum, lses)
    es = [jnp.exp(l - mx) for l in lses]
    den = functools.reduce(lambda a, b: a + b, es)
    weights = [e / den for e in es]
    for h in range(out_ref.shape[2] // HEAD_DIM):
        cols = slice(h * HEAD_DIM, (h + 1) * HEAD_DIM)
        acc = None
        for g, ref in enumerate(o_refs):
            term = weights[g][:, h:h + 1] * token_order(ref, g, cols)
            acc = term if acc is None else acc + term
        out_ref[0, :, cols] = acc.astype(out_ref.dtype)


def _combine(outs, lses, S):
    B = outs[0].shape[0]
    N = outs[0].shape[-1]
    assert HEAD_DIM == LANES
    max_dil = max(o.shape[1] for o in outs)
    tm = _tile(S, 512, SUBLANES_F32 * max_dil)

    def spec(a):
        d = a.shape[1]
        return pl.BlockSpec((1, d, tm // d, a.shape[-1]), lambda b, i: (b, 0, i, 0))

    return pl.pallas_call(
        _combine_kernel,
        grid=(B, S // tm),
        in_specs=[spec(a) for a in (*outs, *lses)],
        out_specs=pl.BlockSpec((1, tm, N), lambda b, i: (b, i, 0)),
        out_shape=jax.ShapeDtypeStruct((B, S, N), BF16),
        scratch_shapes=[pltpu.VMEM((2 * N_GROUPS, tm, LANES), F32)],
        compiler_params=_params("parallel", "arbitrary"),
        name="attn_combine",
    )(*outs, *lses)


def _merge_kernel(x_ref, hc_ref, at_ref, wg0_ref, wg1_ref, wc_ref, wa_ref, b0_ref, b1_ref, o_ref):
    x = x_ref[...]
    g0 = _sigmoid(jnp.dot(x, wg0_ref[...], preferred_element_type=F32) + b0_ref[...])
    g1 = _sigmoid(jnp.dot(x, wg1_ref[...], preferred_element_type=F32) + b1_ref[...])
    conv_out = jnp.dot(hc_ref[...], wc_ref[...], preferred_element_type=F32)
    attn_out = jnp.dot(at_ref[...], wa_ref[...], preferred_element_type=F32)
    o_ref[...] = (g0 * conv_out + g1 * attn_out).astype(o_ref.dtype)


def _merge(xb, hc, attn, w_in, b_in, w_conv_out, w_attn_out, col_gate):
    M, D = xb.shape
    tm = _tile(M, 1024, ROWS_BF16_TILE)
    tn = _tile(math.gcd(D, col_gate), 256)
    off0 = col_gate // tn
    off1 = (col_gate + D) // tn
    return pl.pallas_call(
        _merge_kernel,
        grid=(M // tm, D // tn),
        in_specs=[
            pl.BlockSpec((tm, D), lambda i, j: (i, 0)),
            pl.BlockSpec((tm, hc.shape[1]), lambda i, j: (i, 0)),
            pl.BlockSpec((tm, attn.shape[1]), lambda i, j: (i, 0)),
            pl.BlockSpec((D, tn), lambda i, j: (0, j + off0)),
            pl.BlockSpec((D, tn), lambda i, j: (0, j + off1)),
            pl.BlockSpec((hc.shape[1], tn), lambda i, j: (0, j)),
            pl.BlockSpec((attn.shape[1], tn), lambda i, j: (0, j)),
            pl.BlockSpec((1, tn), lambda i, j: (0, j + off0)),
            pl.BlockSpec((1, tn), lambda i, j: (0, j + off1)),
        ],
        out_specs=pl.BlockSpec((tm, tn), lambda i, j: (i, j)),
        out_shape=jax.ShapeDtypeStruct((M, D), BF16),
        compiler_params=_params("parallel", "arbitrary"),
        name="merge",
    )(xb, hc, attn, w_in, w_in, w_conv_out, w_attn_out, b_in, b_in)


def _zero_rows(ref, n_rows):
    def zero(r):
        ref[pl.ds(r, EPILOGUE_ROWS), :] = jnp.zeros((EPILOGUE_ROWS, ref.shape[1]), ref.dtype)
    _for_row_chunks(n_rows, zero)


def _deferred_norm_steps(n_tiles, acc_refs, main, norm):
    i = pl.program_id(0)
    k = pl.program_id(1)

    @pl.when((i == 0) & (k == 0))
    def _():
        for acc_ref in acc_refs:
            _zero_rows(acc_ref, acc_ref.shape[0])

    @pl.when(i == 0)
    def _():
        main(acc_refs[0])

    for parity in (0, 1):
        @pl.when((i > 0) & (i < n_tiles) & (i % 2 == parity))
        def _():
            norm(acc_refs[1 - parity])
            main(acc_refs[parity])

    @pl.when(i == n_tiles)
    def _():
        norm(acc_refs[(n_tiles - 1) % 2])


def _slab_chunks(k, slab_rows):
    for c in range(slab_rows // EPILOGUE_ROWS):
        r = c * EPILOGUE_ROWS
        yield (pl.ds(pl.multiple_of(k * slab_rows + r, EPILOGUE_ROWS), EPILOGUE_ROWS), pl.ds(r, EPILOGUE_ROWS))


def _deferred_specs(n_tiles, n_steps, tm, D):
    rows = tm // n_steps

    def main_step(i, k):
        return jnp.where(i < n_tiles, k, n_steps - 1)

    def main_tile(i):
        return jnp.minimum(i, n_tiles - 1)

    def a_spec(tk):
        return pl.BlockSpec((tm, tk), lambda i, k: (main_tile(i), main_step(i, k)))

    def w_spec(tk):
        return pl.BlockSpec((tk, D), lambda i, k: (main_step(i, k), 0))

    def main_slab(cols=D):
        return pl.BlockSpec((rows, cols), lambda i, k: (main_tile(i) * n_steps + main_step(i, k), 0))

    def norm_slab(cols=D):
        return pl.BlockSpec((rows, cols), lambda i, k: (jnp.maximum(i - 1, 0) * n_steps + jnp.where(i > 0, k, 0), 0))

    return a_spec, w_spec, main_slab, norm_slab


def _out_proj_kernel(a_ref, w_ref, x_ref, g_ref, b_ref, o32_ref, o16_ref, acc0_ref, acc1_ref, *, alpha, n_tiles):
    k = pl.program_id(1)
    slab_rows = x_ref.shape[0]

    def main(acc_ref):
        _dot_into(acc_ref, [(a_ref[...], w_ref)], accumulate=True)
        for acc_rows, rows in _slab_chunks(k, slab_rows):
            acc_ref[acc_rows, :] = acc_ref[acc_rows, :] + alpha * x_ref[rows, :]

    def norm(acc_ref):
        for acc_rows, rows in _slab_chunks(k, slab_rows):
            y = _layer_norm(acc_ref[acc_rows, :], g_ref[...], b_ref[...])
            o32_ref[rows, :] = y
            o16_ref[rows, :] = y.astype(BF16)
            acc_ref[acc_rows, :] = jnp.zeros_like(y)

    _deferred_norm_steps(n_tiles, (acc0_ref, acc1_ref), main, norm)


def _out_proj(a, w, x, g, b, alpha):
    M, K = a.shape
    D = w.shape[1]
    tk = _tile(K, 1024)
    nk = K // tk
    tm = _tile(M, 512, EPILOGUE_ROWS * nk)
    n_tiles = M // tm
    a_spec, w_spec, main_slab, norm_slab = _deferred_specs(n_tiles, nk, tm, D)
    vec = pl.BlockSpec((1, D), lambda i, k: (0, 0))
    return pl.pallas_call(
        functools.partial(_out_proj_kernel, alpha=alpha, n_tiles=n_tiles),
        grid=(n_tiles + 1, nk),
        in_specs=[a_spec(tk), w_spec(tk), main_slab(), vec, vec],
        out_specs=[norm_slab(), norm_slab()],
        out_shape=[jax.ShapeDtypeStruct((M, D), F32), jax.ShapeDtypeStruct((M, D), BF16)],
        scratch_shapes=[pltpu.VMEM((tm, D), F32), pltpu.VMEM((tm, D), F32)],
        compiler_params=_params("arbitrary", "arbitrary"),
        name="out_proj",
    )(a, w, x, g, b)


def _ple_kernel(a_ref, w_ref, x_ref, p_ref, wp_ref, bg_ref, g_ref, b_ref, o_ref, acc0_ref, acc1_ref,
                *, alpha, n_tiles):
    k = pl.program_id(1)
    slab_rows = x_ref.shape[0]

    def main(acc_ref):
        _dot_into(acc_ref, [(a_ref[...], w_ref)], accumulate=True)

    def norm(acc_ref):
        for acc_rows, rows in _slab_chunks(k, slab_rows):
            gate = _sigmoid(acc_ref[acc_rows, :] + bg_ref[...])
            pw = jnp.dot(p_ref[rows, :].astype(BF16), wp_ref[...], preferred_element_type=F32)
            o_ref[rows, :] = _layer_norm(alpha * x_ref[rows, :] + pw * gate, g_ref[...], b_ref[...])
            acc_ref[acc_rows, :] = jnp.zeros_like(gate)

    _deferred_norm_steps(n_tiles, (acc0_ref, acc1_ref), main, norm)


def _ple(a, w, x, p, wp, bg, g, b, alpha):
    M, K = a.shape
    D = w.shape[1]
    P = p.shape[1]
    tk = _tile(K, 1024)
    nk = K // tk
    tm = _tile(M, 512, EPILOGUE_ROWS * nk)
    n_tiles = M // tm
    a_spec, w_spec, _, norm_slab = _deferred_specs(n_tiles, nk, tm, D)
    vec = pl.BlockSpec((1, D), lambda i, k: (0, 0))
    return pl.pallas_call(
        functools.partial(_ple_kernel, alpha=alpha, n_tiles=n_tiles),
        grid=(n_tiles + 1, nk),
        in_specs=[a_spec(tk), w_spec(tk), norm_slab(), norm_slab(P),
                  pl.BlockSpec((P, D), lambda i, k: (0, 0)), vec, vec, vec],
        out_specs=norm_slab(),
        out_shape=jax.ShapeDtypeStruct((M, D), F32),
        scratch_shapes=[pltpu.VMEM((tm, D), F32), pltpu.VMEM((tm, D), F32)],
        compiler_params=_params("arbitrary", "arbitrary"),
        name="ple",
    )(a, w, x, p, wp, bg, g, b)


def _encoder_layer(x, p, B, S, w, biases, alpha):
    d_conv = w["conv_dw"].shape[1]
    col_q = 2 * d_conv
    col_gate = col_q + 3 * D_ATTN
    ln_g, ln_b = w["ln_g"], w["ln_b"]

    x1, x1b = _ffn(x, x.astype(BF16), w["w_ff1_gate"], w["w_ff1_up"], w["w_ff1_down"], ln_g[0:1], ln_b[0:1], alpha)

    h = _glu_proj(x1b, w["w_in"], w["b_in"], d_conv)
    hc = _conv_module(h.reshape(B, S, d_conv), w["conv_dw"], w["conv_dw_b"], w["conv_ln_g"], w["conv_ln_b"])
    hc = hc.reshape(B * S, d_conv)

    outs, lses = [], []
    for gi, (_, dil) in enumerate(DILATED_GROUPS):
        qkv = _qkv_proj(x1b, w["w_in"], w["b_in"], col_q, B, S, gi, dil)
        o, lse = _attn_group(qkv, biases[gi])
        outs.append(o)
        lses.append(lse)
    attn = _combine(outs, lses, S).reshape(B * S, D_GROUP)

    merged = _merge(x1b, hc, attn, w["w_in"], w["b_in"], w["w_conv_out"], w["w_attn_out"], col_gate)
    x2, x2b = _out_proj(merged, w["w_out"], x1, ln_g[1:2], ln_b[1:2], alpha)
    x3, x3b = _ffn(x2, x2b, w["w_ff2_gate"], w["w_ff2_up"], w["w_ff2_down"], ln_g[2:3], ln_b[2:3], alpha)
    return _ple(x3b, w["w_ple_gate"], x3, p, w["w_ple"], w["b_ple_gate"], ln_g[3:4], ln_b[3:4], alpha)


_MATMUL_WEIGHTS = ("w_ff1_gate", "w_ff1_up", "w_ff1_down", "w_in", "w_conv_out", "w_attn_out", "w_out",
                   "w_ff2_gate", "w_ff2_up", "w_ff2_down", "w_ple", "w_ple_gate")
_ROW_VECTORS = ("b_in", "conv_dw_b", "conv_ln_g", "conv_ln_b", "b_ple_gate")


def kernel(x_prompt, x_sample, p_prompt, p_sample, rel_bias, ln_g, ln_b, w_ff1_gate, w_ff1_up, w_ff1_down,
           w_in, b_in, conv_dw, conv_dw_b, conv_ln_g, conv_ln_b, w_conv_out, w_attn_out, w_out, w_ff2_gate,
           w_ff2_up, w_ff2_down, w_ple, w_ple_gate, b_ple_gate):
    stacked = dict(ln_g=ln_g, ln_b=ln_b, w_ff1_gate=w_ff1_gate, w_ff1_up=w_ff1_up, w_ff1_down=w_ff1_down,
                   w_in=w_in, b_in=b_in, conv_dw=conv_dw, conv_dw_b=conv_dw_b, conv_ln_g=conv_ln_g,
                   conv_ln_b=conv_ln_b, w_conv_out=w_conv_out, w_attn_out=w_attn_out, w_out=w_out,
                   w_ff2_gate=w_ff2_gate, w_ff2_up=w_ff2_up, w_ff2_down=w_ff2_down, w_ple=w_ple,
                   w_ple_gate=w_ple_gate, b_ple_gate=b_ple_gate)
    depth = ln_g.shape[0]
    alpha = (2.0 * depth) ** 0.25
    layers = []
    for i in range(depth):
        w = {k: v[i] for k, v in stacked.items()}
        for k in _MATMUL_WEIGHTS:
            w[k] = w[k].astype(BF16)
        for k in _ROW_VECTORS:
            w[k] = w[k].reshape(1, -1)
        layers.append(w)
    biases = [_band_bias(rel_bias, gi, dil) for gi, (_, dil) in enumerate(DILATED_GROUPS)]

    def run(x, p):
        B, S, D = x.shape
        y = x.reshape(B * S, D)
        for i, w in enumerate(layers):
            y = _encoder_layer(y, p[i].reshape(B * S, -1), B, S, w, biases, alpha)
        return y.reshape(B, S, D)

    return run(x_prompt, p_prompt), run(x_sample, p_sample)
```
